```python
import math
import jax, jax.numpy as jnp
from jax import lax
import numpy as np

D_MODEL = 2048
BATCH = 4
SEQ = 2048
DEPTH = 4
DEC_BATCH = 128
DEC_SEQ = 1
PAST_LEN = 16384
PAGE_SIZE = 128

D_CONV = D_MODEL // 2
D_GDN = D_MODEL - D_CONV
GDN_HEADS = 8
HEAD_DIM = D_GDN // GDN_HEADS
CONV_A_WIDTH = 3
CONV_QKV_WIDTH = 4
CHUNK = 64
EPS = 1e-6
SPLIT_SIZES = (D_CONV, D_CONV, D_CONV, D_CONV, 3 * D_GDN, D_GDN, GDN_HEADS, GDN_HEADS)
SPLIT_POINTS = tuple(int(v) for v in np.cumsum(SPLIT_SIZES)[:-1])
D_IN_PROJ = sum(SPLIT_SIZES)

kernel_name = 'hybrid_shortconv_gdn_adaln_step'


def rmsnorm(x, g):
    xf = x.astype(jnp.float32)
    y = xf * lax.rsqrt(jnp.mean(xf * xf, axis=-1, keepdims=True) + EPS)
    return (y * g.astype(jnp.float32)).astype(x.dtype)


def l2norm(x):
    return x * lax.rsqrt(jnp.sum(x * x, axis=-1, keepdims=True) + EPS)


def causal_depthwise_conv(x, buf, w):
    width = w.shape[0]
    t = x.shape[1]
    xx = jnp.concatenate([buf.astype(x.dtype), x], axis=1)
    y = sum(xx[:, j:j + t] * w[j].astype(x.dtype) for j in range(width))
    return y, xx[:, t:]


def gated_delta_rule(q, k, v, g, beta, s0):
    b, t, h, dk = q.shape
    dv = v.shape[-1]
    f32 = jnp.float32
    c = min(CHUNK, t)
    n = -(-t // c)
    pad = n * c - t

    def prep(a):
        a = a.astype(f32)
        a = jnp.pad(a, [(0, 0), (0, pad)] + [(0, 0)] * (a.ndim - 2))
        a = a.reshape((b, n, c) + a.shape[2:])
        return jnp.moveaxis(a, 3, 1)

    q, k, v, g, beta = prep(q), prep(k), prep(v), prep(g), prep(beta)
    q = q * (dk ** -0.5)
    gc = jnp.cumsum(g, axis=-1)
    idx = jnp.arange(c)
    strict = idx[:, None] > idx[None, :]
    causal = idx[:, None] >= idx[None, :]
    diff = gc[..., :, None] - gc[..., None, :]
    decay_strict = jnp.exp(jnp.where(strict, diff, -jnp.inf))
    decay_causal = jnp.exp(jnp.where(causal, diff, -jnp.inf))
    kb = k * beta[..., None]
    lmat = jnp.einsum('bhnid,bhnjd->bhnij', kb, k) * decay_strict
    eye = jnp.eye(c, dtype=f32)
    tmat = lax.linalg.triangular_solve(eye + lmat, jnp.broadcast_to(eye, lmat.shape),
                                       left_side=True, lower=True, unit_diagonal=True)
    eg = jnp.exp(gc)[..., None]
    u_base = tmat @ (v * beta[..., None])
    w_dec = tmat @ (kb * eg)
    attn = jnp.einsum('bhnid,bhnjd->bhnij', q, k) * decay_causal
    q_dec = q * eg
    k_tail = k * jnp.exp(gc[..., -1:] - gc)[..., None]
    g_last = jnp.exp(gc[..., -1])

    def step(s, xs):
        u_b, w_c, a_c, q_c, kt_c, gl_c = xs
        u = u_b - w_c @ s
        o = q_c @ s + a_c @ u
        s = s * gl_c[..., None, None] + jnp.einsum('bhcd,bhce->bhde', kt_c, u)
        return s, o

    xs = tuple(jnp.moveaxis(a, 2, 0) for a in (u_base, w_dec, attn, q_dec, k_tail, g_last))
    s, o = lax.scan(step, s0.astype(f32), xs)
    o = jnp.moveaxis(o, 0, 2).reshape(b, h, n * c, dv)[:, :, :t]
    return jnp.moveaxis(o, 1, 2), s


def mixer_layer(x, c, conv_a_buf, conv_qkv_buf, s0, norm_g, w_ada, b_ada, w_in, conv_a_w,
                conv_qkv_w, a_log, dt_bias, o_norm_g, w_out):
    f32 = jnp.float32
    bsz, t, _ = x.shape
    mod = jax.nn.silu(c) @ w_ada + b_ada
    shift, scale, gate = jnp.split(mod[:, None, :], 3, axis=-1)
    h = rmsnorm(x, norm_g) * (1.0 + scale) + shift
    z = h @ w_in
    b_a, c_a, h_a, g_a, qkv, g_b, beta_logit, alpha_logit = jnp.split(z, SPLIT_POINTS, axis=-1)
    conv_out, new_a_buf = causal_depthwise_conv(c_a * h_a, conv_a_buf, conv_a_w)
    y_a = b_a * conv_out * jax.nn.silu(g_a)
    qkv_c, new_qkv_buf = causal_depthwise_conv(qkv, conv_qkv_buf, conv_qkv_w)
    qkv_c = jax.nn.silu(qkv_c).astype(f32)
    q, k, v = [a.reshape(bsz, t, GDN_HEADS, HEAD_DIM) for a in jnp.split(qkv_c, 3, axis=-1)]
    q, k = l2norm(q), l2norm(k)
    beta = jax.nn.sigmoid(beta_logit.astype(f32))
    g = -jnp.exp(a_log.astype(f32)) * jax.nn.softplus(alpha_logit.astype(f32) + dt_bias.astype(f32))
    o, s = gated_delta_rule(q, k, v, g, beta, s0)
    o = rmsnorm(o, o_norm_g).reshape(bsz, t, D_GDN).astype(x.dtype) * jax.nn.silu(g_b)
    y = jnp.concatenate([y_a, o], axis=-1) @ w_out
    return x + gate * y, new_a_buf, new_qkv_buf, s.astype(s0.dtype)


def trunk(x, c, conv_a0, conv_qkv0, ssm0, norm_g, w_ada, b_ada, w_in, conv_a_w, conv_qkv_w,
          a_log, dt_bias, o_norm_g, w_out, final_norm_g):
    new_a, new_qkv, new_s = [], [], []
    for i in range(DEPTH):
        x, a_buf, qkv_buf, s = mixer_layer(x, c, conv_a0[i], conv_qkv0[i], ssm0[i], norm_g[i],
                                           w_ada[i], b_ada[i], w_in[i], conv_a_w[i], conv_qkv_w[i],
                                           a_log[i], dt_bias[i], o_norm_g[i], w_out[i])
        new_a.append(a_buf)
        new_qkv.append(qkv_buf)
        new_s.append(s)
    return rmsnorm(x, final_norm_g), jnp.stack(new_a), jnp.stack(new_qkv), jnp.stack(new_s)


def setup_inputs(seed: int = 0) -> dict:
    key = jax.random.key(seed)
    ks = jax.random.split(key, 18)
    f32 = jnp.float32

    def nrm(k, shape, s):
        return jax.random.normal(k, shape, f32) * s

    x_prompt = nrm(ks[0], (BATCH, SEQ, D_MODEL), 1.0)
    x_sample = nrm(ks[1], (DEC_BATCH, DEC_SEQ, D_MODEL), 1.0)
    state_conv_a = nrm(ks[2], (DEPTH, DEC_BATCH, CONV_A_WIDTH - 1, D_CONV), 1.0)
    state_conv_qkv = nrm(ks[3], (DEPTH, DEC_BATCH, CONV_QKV_WIDTH - 1, 3 * D_GDN), 1.0)
    state_ssm = nrm(ks[4], (DEPTH, DEC_BATCH, GDN_HEADS, HEAD_DIM, HEAD_DIM), 0.1)
    c_prompt = nrm(ks[5], (BATCH, D_MODEL), 1.0)
    c_sample = nrm(ks[6], (DEC_BATCH, D_MODEL), 1.0)
    norm_g = 1.0 + nrm(ks[7], (DEPTH, D_MODEL), 0.02)
    w_ada = nrm(ks[8], (DEPTH, D_MODEL, 3 * D_MODEL), 0.5 * D_MODEL ** -0.5)
    b_ada = nrm(ks[9], (DEPTH, 3 * D_MODEL), 0.02)
    w_in = nrm(ks[10], (DEPTH, D_MODEL, D_IN_PROJ), D_MODEL ** -0.5)
    conv_a_w = nrm(ks[11], (DEPTH, CONV_A_WIDTH, D_CONV), CONV_A_WIDTH ** -0.5)
    conv_qkv_w = nrm(ks[12], (DEPTH, CONV_QKV_WIDTH, 3 * D_GDN), CONV_QKV_WIDTH ** -0.5)
    a_log = jnp.log(jax.random.uniform(ks[13], (DEPTH, GDN_HEADS), f32, 1.0, 16.0))
    dt = jnp.exp(jax.random.uniform(ks[14], (DEPTH, GDN_HEADS), f32, math.log(1e-3), math.log(1e-1)))
    dt_bias = dt + jnp.log(-jnp.expm1(-dt))
    o_norm_g = 1.0 + nrm(ks[15], (DEPTH, HEAD_DIM), 0.02)
    w_out = nrm(ks[16], (DEPTH, D_MODEL, D_MODEL), D_MODEL ** -0.5)
    final_norm_g = 1.0 + nrm(ks[17], (D_MODEL,), 0.02)
    return {'x_prompt': x_prompt, 'x_sample': x_sample, 'state_conv_a': state_conv_a,
            'state_conv_qkv': state_conv_qkv, 'state_ssm': state_ssm, 'c_prompt': c_prompt,
            'c_sample': c_sample, 'norm_g': norm_g, 'w_ada': w_ada, 'b_ada': b_ada, 'w_in': w_in,
            'conv_a_w': conv_a_w, 'conv_qkv_w': conv_qkv_w, 'a_log': a_log, 'dt_bias': dt_bias,
            'o_norm_g': o_norm_g, 'w_out': w_out, 'final_norm_g': final_norm_g}


def reference(x_prompt, x_sample, state_conv_a, state_conv_qkv, state_ssm, c_prompt, c_sample,
              norm_g, w_ada, b_ada, w_in, conv_a_w, conv_qkv_w, a_log, dt_bias, o_norm_g, w_out,
              final_norm_g):
    bp = x_prompt.shape[0]
    zeros_a = jnp.zeros((DEPTH, bp, CONV_A_WIDTH - 1, D_CONV), x_prompt.dtype)
    zeros_qkv = jnp.zeros((DEPTH, bp, CONV_QKV_WIDTH - 1, 3 * D_GDN), x_prompt.dtype)
    zeros_s = jnp.zeros((DEPTH, bp, GDN_HEADS, HEAD_DIM, HEAD_DIM), state_ssm.dtype)
    y_prompt, conv_a_p, conv_qkv_p, ssm_p = trunk(
        x_prompt, c_prompt, zeros_a, zeros_qkv, zeros_s, norm_g, w_ada, b_ada, w_in, conv_a_w,
        conv_qkv_w, a_log, dt_bias, o_norm_g, w_out, final_norm_g)
    y_sample, conv_a_s, conv_qkv_s, ssm_s = trunk(
        x_sample, c_sample, state_conv_a, state_conv_qkv, state_ssm, norm_g, w_ada, b_ada, w_in,
        conv_a_w, conv_qkv_w, a_log, dt_bias, o_norm_g, w_out, final_norm_g)
    return (y_prompt, y_sample, conv_a_p, conv_qkv_p, ssm_p, conv_a_s, conv_qkv_s, ssm_s)
```

```python
import functools
import math

import jax
import jax.numpy as jnp
from jax import lax
from jax.experimental import pallas as pl
from jax.experimental.pallas import tpu as pltpu

F32 = jnp.float32
BF16 = jnp.bfloat16

D_MODEL = 2048
DEPTH = 4
D_CONV = 1024
D_GDN = 1024
HEADS = 8
HEAD_DIM = 128
CHUNK = 64
EPS = 1e-6
D_MAIN = 8192
OFF_BA, OFF_CA, OFF_HA, OFF_GA, OFF_QKV, OFF_GB = 0, 1024, 2048, 3072, 4096, 7168
LANES = 128
HALO = 8
VMEM_LIMIT = 56 * 1024 * 1024


def _silu(x):
    return x * jax.nn.sigmoid(x)


def _softplus(x):
    return jnp.maximum(x, 0.0) + jnp.log1p(jnp.exp(-jnp.abs(x)))


def _dot(a, b):
    return jnp.dot(a, b, preferred_element_type=F32)


def _dot_nt(a, b):
    return lax.dot_general(a, b, (((1,), (1,)), ((), ())), preferred_element_type=F32)


def _dot_tn(a, b):
    return lax.dot_general(a, b, (((0,), (0,)), ((), ())), preferred_element_type=F32)


def _dot_hi(a, b):
    return jnp.dot(a, b, preferred_element_type=F32, precision=lax.Precision.HIGHEST)


def _mod_kernel(c_ref, w_ref, b_ref, o_ref):
    s = _silu(c_ref[...]).astype(BF16)
    o_ref[...] = _dot(s, w_ref[...].astype(BF16)) + b_ref[...]


def _mod_call(c_all, w_ada, b_ada):
    rows = c_all.shape[0]
    tn = 1024
    return pl.pallas_call(
        _mod_kernel,
        grid=(DEPTH, 3 * D_MODEL // tn),
        in_specs=[
            pl.BlockSpec((rows, D_MODEL), lambda l, j: (0, 0)),
            pl.BlockSpec((None, D_MODEL, tn), lambda l, j: (l, 0, j)),
            pl.BlockSpec((None, 1, tn), lambda l, j: (l, 0, j)),
        ],
        out_specs=pl.BlockSpec((None, rows, tn), lambda l, j: (l, 0, j)),
        out_shape=jax.ShapeDtypeStruct((DEPTH, rows, 3 * D_MODEL), F32),
        compiler_params=pltpu.CompilerParams(
            dimension_semantics=("arbitrary", "arbitrary"), vmem_limit_bytes=VMEM_LIMIT),
        name="adaln_mod",
    )(c_all, w_ada, b_ada.reshape(DEPTH, 1, 3 * D_MODEL))


def _inproj_kernel(x_ref, g_ref, sc_ref, sh_ref, w_ref, wt_ref, z_ref, zt_ref, h_scr):
    @pl.when(pl.program_id(1) == 0)
    def _():
        x = x_ref[...]
        y = x * lax.rsqrt(jnp.mean(x * x, axis=-1, keepdims=True) + EPS) * g_ref[...]
        h = (y * (1.0 + sc_ref[...]) + sh_ref[...]).astype(BF16)
        h_scr[...] = h
        zt_ref[...] = _dot(h, wt_ref[...])

    z_ref[...] = _dot(h_scr[...], w_ref[...])


def _inproj_call(layer, x, norm_g3, mod_arr, mod_specs, w_main, w_tail, tm):
    rows = x.shape[0]
    tn = 1024
    sc_spec, sh_spec = mod_specs
    return pl.pallas_call(
        _inproj_kernel,
        grid=(rows // tm, D_MAIN // tn),
        in_specs=[
            pl.BlockSpec((tm, D_MODEL), lambda i, j: (i, 0)),
            pl.BlockSpec((None, 1, D_MODEL), lambda i, j: (layer, 0, 0)),
            sc_spec,
            sh_spec,
            pl.BlockSpec((None, D_MODEL, tn), lambda i, j: (layer, 0, j)),
            pl.BlockSpec((None, D_MODEL, LANES), lambda i, j: (layer, 0, 0)),
        ],
        out_specs=[
            pl.BlockSpec((tm, tn), lambda i, j: (i, j)),
            pl.BlockSpec((tm, LANES), lambda i, j: (i, 0)),
        ],
        out_shape=[
            jax.ShapeDtypeStruct((rows, D_MAIN), F32),
            jax.ShapeDtypeStruct((rows, LANES), F32),
        ],
        scratch_shapes=[pltpu.VMEM((tm, D_MODEL), BF16)],
        compiler_params=pltpu.CompilerParams(
            dimension_semantics=("arbitrary", "arbitrary"), vmem_limit_bytes=VMEM_LIMIT),
        name="in_proj",
    )(x, norm_g3, mod_arr, mod_arr, w_main, w_tail)


def _outproj_kernel(y_ref, x_ref, gate_ref, w_ref, fg_ref, o_ref, *, final):
    acc = _dot(y_ref[...].astype(BF16), w_ref[...])
    xn = x_ref[...] + gate_ref[...] * acc
    if final:
        xn = xn * lax.rsqrt(jnp.mean(xn * xn, axis=-1, keepdims=True) + EPS) * fg_ref[...]
    o_ref[...] = xn


def _outproj_call(layer, y, x, mod_arr, gate_spec, w_out, final_g, tm):
    rows = x.shape[0]
    return pl.pallas_call(
        functools.partial(_outproj_kernel, final=(layer == DEPTH - 1)),
        grid=(rows // tm,),
        in_specs=[
            pl.BlockSpec((tm, D_MODEL), lambda i: (i, 0)),
            pl.BlockSpec((tm, D_MODEL), lambda i: (i, 0)),
            gate_spec,
            pl.BlockSpec((None, D_MODEL, D_MODEL), lambda i: (layer, 0, 0)),
            pl.BlockSpec((1, D_MODEL), lambda i: (0, 0)),
        ],
        out_specs=pl.BlockSpec((tm, D_MODEL), lambda i: (i, 0)),
        out_shape=jax.ShapeDtypeStruct((rows, D_MODEL), F32),
        compiler_params=pltpu.CompilerParams(
            dimension_semantics=("arbitrary",), vmem_limit_bytes=VMEM_LIMIT),
        name="out_proj",
    )(y, x, mod_arr, w_out, final_g)


def _gate_terms(tail, alog_row, dtb_row):
    beta = jax.nn.sigmoid(tail)
    g = -jnp.exp(alog_row) * _softplus(tail + dtb_row)
    return beta, g


def _unit_lower_inverse(lmat, eye, lane_lt_c, zeros_top):
    w = jnp.concatenate([eye, -lmat], axis=1)
    for _ in range(6):
        zmat = jnp.concatenate([zeros_top, w], axis=0)
        w = jnp.where(lane_lt_c, w, 0.0) + _dot_hi(w, zmat)
    return w[:, :CHUNK]


def _mixer_prompt_kernel(z_ref, zt_ref, wa_ref, wq_ref, alog_ref, dtb_ref, on_ref,
                         y_ref, ca_ref, cq_ref, ss_ref, buf_a, buf_q, s_scr):
    c_idx = pl.program_id(1)
    C = CHUNK

    @pl.when(c_idx == 0)
    def _():
        buf_a[0:HALO, :] = jnp.zeros((HALO, D_CONV), F32)
        buf_q[0:HALO, :] = jnp.zeros((HALO, 3 * D_GDN), F32)
        s_scr[...] = jnp.zeros(s_scr.shape, F32)

    buf_a[HALO:HALO + C, :] = z_ref[:, OFF_CA:OFF_CA + D_CONV] * z_ref[:, OFF_HA:OFF_HA + D_CONV]
    conv = (buf_a[HALO - 2:HALO - 2 + C, :] * wa_ref[0:1, :]
            + buf_a[HALO - 1:HALO - 1 + C, :] * wa_ref[1:2, :]
            + buf_a[HALO:HALO + C, :] * wa_ref[2:3, :])
    ya = z_ref[:, OFF_BA:OFF_BA + D_CONV] * conv * _silu(z_ref[:, OFF_GA:OFF_GA + D_CONV])
    y_ref[:, 0:D_CONV] = ya.astype(y_ref.dtype)
    ca_ref[...] = buf_a[HALO + C - 2:HALO + C, :]
    buf_a[0:HALO, :] = buf_a[C:C + HALO, :]

    buf_q[HALO:HALO + C, :] = z_ref[:, OFF_QKV:OFF_QKV + 3 * D_GDN]
    cq_ref[...] = buf_q[HALO + C - 3:HALO + C, :]

    beta_all, g_all = _gate_terms(zt_ref[...], alog_ref[...], dtb_ref[...])
    row = lax.broadcasted_iota(jnp.int32, (C, C), 0)
    col = lax.broadcasted_iota(jnp.int32, (C, C), 1)
    causal = row >= col
    strict = row > col
    eye = (row == col).astype(F32)
    gc_all = _dot_hi(causal.astype(F32), g_all)
    gc_t = gc_all.T
    eg_all = jnp.exp(gc_all)
    gc_last = gc_all[C - 1:C, :]
    kt_all = jnp.exp(gc_last - gc_all)
    gl_all = jnp.exp(gc_last)
    lane_lt_c = lax.broadcasted_iota(jnp.int32, (C, 2 * C), 1) < C
    zeros_top = jnp.zeros((C, 2 * C), F32)

    def conv_qkv(off):
        acc = buf_q[HALO - 3:HALO - 3 + C, off:off + HEAD_DIM] * wq_ref[0:1, off:off + HEAD_DIM]
        for j in range(1, 4):
            acc = acc + (buf_q[HALO - 3 + j:HALO - 3 + j + C, off:off + HEAD_DIM]
                         * wq_ref[j:j + 1, off:off + HEAD_DIM])
        return _silu(acc)

    for h in range(HEADS):
        gl = HEADS + h
        q = conv_qkv(h * HEAD_DIM)
        k = conv_qkv(D_GDN + h * HEAD_DIM)
        v = conv_qkv(2 * D_GDN + h * HEAD_DIM)
        q = q * (lax.rsqrt(jnp.sum(q * q, axis=-1, keepdims=True) + EPS) * (HEAD_DIM ** -0.5))
        k = k * lax.rsqrt(jnp.sum(k * k, axis=-1, keepdims=True) + EPS)
        beta = beta_all[:, h:h + 1]
        eg = eg_all[:, gl:gl + 1]
        diff = gc_all[:, gl:gl + 1] - gc_t[gl:gl + 1, :]
        dec_causal = jnp.exp(jnp.where(causal, diff, -jnp.inf))
        dec_strict = jnp.where(strict, dec_causal, 0.0)
        kb = k * beta
        k16 = k.astype(BF16)
        lmat = _dot_nt(kb.astype(BF16), k16) * dec_strict
        attn = _dot_nt(q.astype(BF16), k16) * dec_causal
        tmat = _unit_lower_inverse(lmat, eye, lane_lt_c, zeros_top)
        rhs = jnp.concatenate([v * beta, kb * eg], axis=1).astype(BF16)
        uw = _dot(tmat.astype(BF16), rhs)
        s_old = s_scr[h]
        lhs = jnp.concatenate([uw[:, HEAD_DIM:], q * eg], axis=0).astype(BF16)
        ws = _dot(lhs, s_old.astype(BF16))
        u = uw[:, :HEAD_DIM] - ws[:C]
        u16 = u.astype(BF16)
        o = ws[C:] + _dot(attn.astype(BF16), u16)
        k_tail = (k * kt_all[:, gl:gl + 1]).astype(BF16)
        s_scr[h] = s_old * gl_all[:, gl:gl + 1] + _dot_tn(k_tail, u16)
        o = o * lax.rsqrt(jnp.mean(o * o, axis=-1, keepdims=True) + EPS) * on_ref[...]
        gb = z_ref[:, OFF_GB + h * HEAD_DIM:OFF_GB + (h + 1) * HEAD_DIM]
        y_ref[:, D_CONV + h * HEAD_DIM:D_CONV + (h + 1) * HEAD_DIM] = (o * _silu(gb)).astype(y_ref.dtype)

    buf_q[0:HALO, :] = buf_q[C:C + HALO, :]

    @pl.when(c_idx == pl.num_programs(1) - 1)
    def _():
        ss_ref[...] = s_scr[...]


def _mixer_prompt_call(layer, z, zt, conv_a_w, conv_qkv_w, alog_rows, dtb_rows, onorm3, batch, seq):
    nc = seq // CHUNK
    lsel = lambda b, c: (layer, 0, 0)
    return pl.pallas_call(
        _mixer_prompt_kernel,
        grid=(batch, nc),
        in_specs=[
            pl.BlockSpec((CHUNK, D_MAIN), lambda b, c: (b * nc + c, 0)),
            pl.BlockSpec((CHUNK, LANES), lambda b, c: (b * nc + c, 0)),
            pl.BlockSpec((None, 3, D_CONV), lsel),
            pl.BlockSpec((None, 4, 3 * D_GDN), lsel),
            pl.BlockSpec((None, 1, LANES), lsel),
            pl.BlockSpec((None, 1, LANES), lsel),
            pl.BlockSpec((None, 1, HEAD_DIM), lsel),
        ],
        out_specs=[
            pl.BlockSpec((CHUNK, D_MODEL), lambda b, c: (b * nc + c, 0)),
            pl.BlockSpec((None, 2, D_CONV), lambda b, c: (b, 0, 0)),
            pl.BlockSpec((None, 3, 3 * D_GDN), lambda b, c: (b, 0, 0)),
            pl.BlockSpec((None, HEADS, HEAD_DIM, HEAD_DIM), lambda b, c: (b, 0, 0, 0)),
        ],
        out_shape=[
            jax.ShapeDtypeStruct((batch * seq, D_MODEL), BF16),
            jax.ShapeDtypeStruct((batch, 2, D_CONV), F32),
            jax.ShapeDtypeStruct((batch, 3, 3 * D_GDN), F32),
            jax.ShapeDtypeStruct((batch, HEADS, HEAD_DIM, HEAD_DIM), F32),
        ],
        scratch_shapes=[
            pltpu.VMEM((HALO + CHUNK, D_CONV), F32),
            pltpu.VMEM((HALO + CHUNK, 3 * D_GDN), F32),
            pltpu.VMEM((HEADS, HEAD_DIM, HEAD_DIM), F32),
        ],
        compiler_params=pltpu.CompilerParams(
            dimension_semantics=("arbitrary", "arbitrary"), vmem_limit_bytes=VMEM_LIMIT),
        name="mixer_prompt",
    )(z, zt, conv_a_w, conv_qkv_w, alog_rows, dtb_rows, onorm3)


def _mixer_sample_kernel(z_ref, zt_ref, sa_ref, sq_ref, s_ref, wa_ref, wq_ref, alog_ref, dtb_ref,
                         on_ref, y_ref, na_ref, nq_ref, ns_ref):
    nb = z_ref.shape[0]

    ch = z_ref[:, OFF_CA:OFF_CA + D_CONV] * z_ref[:, OFF_HA:OFF_HA + D_CONV]
    prev0 = sa_ref[:, 0:D_CONV]
    prev1 = sa_ref[:, D_CONV:2 * D_CONV]
    conv = prev0 * wa_ref[0:1, :] + prev1 * wa_ref[1:2, :] + ch * wa_ref[2:3, :]
    y_ref[:, 0:D_CONV] = (z_ref[:, OFF_BA:OFF_BA + D_CONV] * conv
                          * _silu(z_ref[:, OFF_GA:OFF_GA + D_CONV]))
    na_ref[:, 0:D_CONV] = prev1
    na_ref[:, D_CONV:2 * D_CONV] = ch

    w3 = 3 * D_GDN
    nq_ref[:, 0:w3] = sq_ref[:, w3:2 * w3]
    nq_ref[:, w3:2 * w3] = sq_ref[:, 2 * w3:3 * w3]
    nq_ref[:, 2 * w3:3 * w3] = z_ref[:, OFF_QKV:OFF_QKV + w3]

    beta_all, g_all = _gate_terms(zt_ref[...], alog_ref[...], dtb_ref[...])
    eg_all = jnp.exp(g_all)
    rowid = lax.broadcasted_iota(jnp.int32, (nb, HEAD_DIM), 0)

    def conv_qkv(off):
        sl = slice(off, off + HEAD_DIM)
        acc = z_ref[:, OFF_QKV + off:OFF_QKV + off + HEAD_DIM] * wq_ref[3:4, sl]
        for j in range(3):
            acc = acc + sq_ref[:, j * w3 + off:j * w3 + off + HEAD_DIM] * wq_ref[j:j + 1, sl]
        return _silu(acc)

    for h in range(HEADS):
        q = conv_qkv(h * HEAD_DIM)
        k = conv_qkv(D_GDN + h * HEAD_DIM)
        v = conv_qkv(2 * D_GDN + h * HEAD_DIM)
        q = q * (lax.rsqrt(jnp.sum(q * q, axis=-1, keepdims=True) + EPS) * (HEAD_DIM ** -0.5))
        k = k * lax.rsqrt(jnp.sum(k * k, axis=-1, keepdims=True) + EPS)
        beta = beta_all[:, h:h + 1]
        eg = eg_all[:, HEADS + h:HEADS + h + 1]
        kq = jnp.concatenate([k, q], axis=0)
        ks = jnp.zeros((nb, HEAD_DIM), F32)
        qs = jnp.zeros((nb, HEAD_DIM), F32)
        for j in range(nb):
            r = _dot(kq, s_ref[j, h])
            ks = jnp.where(rowid == j, r[:nb], ks)
            qs = jnp.where(rowid == j, r[nb:], qs)
        u = beta * (v - eg * ks)
        o = eg * qs + jnp.sum(q * k, axis=-1, keepdims=True) * u
        for j in range(nb):
            outer = _dot_tn(jnp.where(rowid == j, k, 0.0), u)
            ns_ref[j, h] = s_ref[j, h] * eg[j:j + 1, :] + outer
        o = o * lax.rsqrt(jnp.mean(o * o, axis=-1, keepdims=True) + EPS) * on_ref[...]
        gb = z_ref[:, OFF_GB + h * HEAD_DIM:OFF_GB + (h + 1) * HEAD_DIM]
        y_ref[:, D_CONV + h * HEAD_DIM:D_CONV + (h + 1) * HEAD_DIM] = o * _silu(gb)


def _mixer_sample_call(layer, z, zt, sa, sq, state_ssm, conv_a_w, conv_qkv_w, alog_rows, dtb_rows,
                       onorm3):
    nseq = z.shape[0]
    nb = 8
    lsel = lambda i: (layer, 0, 0)
    return pl.pallas_call(
        _mixer_sample_kernel,
        grid=(nseq // nb,),
        in_specs=[
            pl.BlockSpec((nb, D_MAIN), lambda i: (i, 0)),
            pl.BlockSpec((nb, LANES), lambda i: (i, 0)),
            pl.BlockSpec((nb, 2 * D_CONV), lambda i: (i, 0)),
            pl.BlockSpec((nb, 9 * D_GDN), lambda i: (i, 0)),
            pl.BlockSpec((None, nb, HEADS, HEAD_DIM, HEAD_DIM), lambda i: (layer, i, 0, 0, 0)),
            pl.BlockSpec((None, 3, D_CONV), lsel),
            pl.BlockSpec((None, 4, 3 * D_GDN), lsel),
            pl.BlockSpec((None, 1, LANES), lsel),
            pl.BlockSpec((None, 1, LANES), lsel),
            pl.BlockSpec((None, 1, HEAD_DIM), lsel),
        ],
        out_specs=[
            pl.BlockSpec((nb, D_MODEL), lambda i: (i, 0)),
            pl.BlockSpec((nb, 2 * D_CONV), lambda i: (i, 0)),
            pl.BlockSpec((nb, 9 * D_GDN), lambda i: (i, 0)),
            pl.BlockSpec((nb, HEADS, HEAD_DIM, HEAD_DIM), lambda i: (i, 0, 0, 0)),
        ],
        out_shape=[
            jax.ShapeDtypeStruct((nseq, D_MODEL), F32),
            jax.ShapeDtypeStruct((nseq, 2 * D_CONV), F32),
            jax.ShapeDtypeStruct((nseq, 9 * D_GDN), F32),
            jax.ShapeDtypeStruct((nseq, HEADS, HEAD_DIM, HEAD_DIM), F32),
        ],
        compiler_params=pltpu.CompilerParams(
            dimension_semantics=("arbitrary",), vmem_limit_bytes=VMEM_LIMIT),
        name="mixer_sample",
    )(z, zt, sa, sq, state_ssm, conv_a_w, conv_qkv_w, alog_rows, dtb_rows, onorm3)


def kernel(x_prompt, x_sample, state_conv_a, state_conv_qkv, state_ssm, c_prompt, c_sample, norm_g, w_ada, b_ada, w_in, conv_a_w, conv_qkv_w, a_log, dt_bias, o_norm_g, w_out, final_norm_g):
    batch, seq, _ = x_prompt.shape
    nseq = x_sample.shape[0]
    assert x_sample.shape[1] == 1 and seq % CHUNK == 0

    w_main = w_in[:, :, :D_MAIN].astype(BF16)
    w_tail = jnp.pad(w_in[:, :, D_MAIN:], ((0, 0), (0, 0), (0, LANES - 2 * HEADS))).astype(BF16)
    w_out16 = w_out.astype(BF16)
    norm_g3 = norm_g.reshape(DEPTH, 1, D_MODEL)
    onorm3 = o_norm_g.reshape(DEPTH, 1, HEAD_DIM)
    final_g = final_norm_g.reshape(1, D_MODEL)
    pad_heads = ((0, 0), (HEADS, LANES - 2 * HEADS))
    alog_rows = jnp.pad(a_log, pad_heads).reshape(DEPTH, 1, LANES)
    dtb_rows = jnp.pad(dt_bias, pad_heads).reshape(DEPTH, 1, LANES)

    n_cond = nseq + batch
    n_cond_pad = -(-n_cond // 8) * 8
    c_all = jnp.concatenate(
        [c_sample, c_prompt, jnp.zeros((n_cond_pad - n_cond, D_MODEL), F32)], axis=0)
    mod = _mod_call(c_all, w_ada, b_ada)
    mod4 = mod.reshape(DEPTH, n_cond_pad, 1, 3 * D_MODEL)

    xp = x_prompt.reshape(batch * seq, D_MODEL)
    xs = x_sample.reshape(nseq, D_MODEL)
    tm_in = 1024
    tm_out = 512
    outs = {k: [] for k in ("ca_p", "cq_p", "ss_p", "ca_s", "cq_s", "ss_s")}

    for layer in range(DEPTH):
        def pmod(sec, per_batch, layer=layer):
            return pl.BlockSpec((None, None, 1, D_MODEL),
                                lambda i, *_: (layer, nseq + i // per_batch, 0, sec))
        z, zt = _inproj_call(layer, xp, norm_g3, mod4,
                             (pmod(1, seq // tm_in), pmod(0, seq // tm_in)), w_main, w_tail, tm_in)
        y, ca, cq, ss = _mixer_prompt_call(layer, z, zt, conv_a_w, conv_qkv_w, alog_rows, dtb_rows,
                                           onorm3, batch, seq)
        xp = _outproj_call(layer, y, xp, mod4, pmod(2, seq // tm_out), w_out16, final_g, tm_out)
        outs["ca_p"].append(ca)
        outs["cq_p"].append(cq)
        outs["ss_p"].append(ss)

        def smod(sec, layer=layer):
            return pl.BlockSpec((None, nseq, D_MODEL), lambda i, *_: (layer, 0, sec))
        z, zt = _inproj_call(layer, xs, norm_g3, mod, (smod(1), smod(0)), w_main, w_tail, nseq)
        sa = state_conv_a[layer].reshape(nseq, 2 * D_CONV)
        sq = state_conv_qkv[layer].reshape(nseq, 9 * D_GDN)
        y, na, nq, ns = _mixer_sample_call(layer, z, zt, sa, sq, state_ssm, conv_a_w, conv_qkv_w,
                                           alog_rows, dtb_rows, onorm3)
        xs = _outproj_call(layer, y, xs, mod, smod(2), w_out16, final_g, nseq)
        outs["ca_s"].append(na.reshape(nseq, 2, D_CONV))
        outs["cq_s"].append(nq.reshape(nseq, 3, 3 * D_GDN))
        outs["ss_s"].append(ns)

    return (xp.reshape(batch, seq, D_MODEL), xs.reshape(nseq, 1, D_MODEL),
            jnp.stack(outs["ca_p"]), jnp.stack(outs["cq_p"]), jnp.stack(outs["ss_p"]),
            jnp.stack(outs["ca_s"]), jnp.stack(outs["cq_s"]), jnp.stack(outs["ss_s"]))
```

```python
import functools
import math

import jax
import jax.numpy as jnp
from jax import lax
from jax.experimental import pallas as pl
from jax.experimental.pallas import tpu as pltpu

F32 = jnp.float32
BF16 = jnp.bfloat16

D_MODEL = 2048
DEPTH = 4
D_CONV = 1024
D_GDN = 1024
HEADS = 8
HEAD_DIM = 128
CHUNK = 64
EPS = 1e-6
D_MAIN = 8192
OFF_BA, OFF_CA, OFF_HA, OFF_GA, OFF_QKV, OFF_GB = 0, 1024, 2048, 3072, 4096, 7168
LANES = 128
HALO = 8
VMEM_LIMIT = 56 * 1024 * 1024


def _silu(x):
    return x * jax.nn.sigmoid(x)


def _softplus(x):
    return jnp.maximum(x, 0.0) + jnp.log1p(jnp.exp(-jnp.abs(x)))


def _dot(a, b):
    return jnp.dot(a, b, preferred_element_type=F32)


def _dot_nt(a, b):
    return lax.dot_general(a, b, (((1,), (1,)), ((), ())), preferred_element_type=F32)


def _dot_tn(a, b):
    return lax.dot_general(a, b, (((0,), (0,)), ((), ())), preferred_element_type=F32)


def _dot_hi(a, b):
    return jnp.dot(a, b, preferred_element_type=F32, precision=lax.Precision.HIGHEST)


def _mod_kernel(c_ref, w_ref, b_ref, o_ref):
    s = _silu(c_ref[...]).astype(BF16)
    o_ref[...] = _dot(s, w_ref[...].astype(BF16)) + b_ref[...]


def _mod_call(c_all, w_ada, b_ada):
    rows = c_all.shape[0]
    tn = 1024
    return pl.pallas_call(
        _mod_kernel,
        grid=(DEPTH, 3 * D_MODEL // tn),
        in_specs=[
            pl.BlockSpec((rows, D_MODEL), lambda l, j: (0, 0)),
            pl.BlockSpec((None, D_MODEL, tn), lambda l, j: (l, 0, j)),
            pl.BlockSpec((None, 1, tn), lambda l, j: (l, 0, j)),
        ],
        out_specs=pl.BlockSpec((None, rows, tn), lambda l, j: (l, 0, j)),
        out_shape=jax.ShapeDtypeStruct((DEPTH, rows, 3 * D_MODEL), F32),
        compiler_params=pltpu.CompilerParams(
            dimension_semantics=("arbitrary", "arbitrary"), vmem_limit_bytes=VMEM_LIMIT),
        name="adaln_mod",
    )(c_all, w_ada, b_ada.reshape(DEPTH, 1, 3 * D_MODEL))


def _inproj_kernel(x_ref, g_ref, sc_ref, sh_ref, w_ref, wt_ref, z_ref, zt_ref, h_scr):
    @pl.when(pl.program_id(1) == 0)
    def _():
        x = x_ref[...]
        y = x * lax.rsqrt(jnp.mean(x * x, axis=-1, keepdims=True) + EPS) * g_ref[...]
        h = (y * (1.0 + sc_ref[...]) + sh_ref[...]).astype(BF16)
        h_scr[...] = h
        zt_ref[...] = _dot(h, wt_ref[...])

    z_ref[...] = _dot(h_scr[...], w_ref[...])


def _inproj_call(layer, x, norm_g3, mod_arr, mod_specs, w_main, w_tail, tm):
    rows = x.shape[0]
    tn = 1024
    sc_spec, sh_spec = mod_specs
    return pl.pallas_call(
        _inproj_kernel,
        grid=(rows // tm, D_MAIN // tn),
        in_specs=[
            pl.BlockSpec((tm, D_MODEL), lambda i, j: (i, 0)),
            pl.BlockSpec((None, 1, D_MODEL), lambda i, j: (layer, 0, 0)),
            sc_spec,
            sh_spec,
            pl.BlockSpec((None, D_MODEL, tn), lambda i, j: (layer, 0, j)),
            pl.BlockSpec((None, D_MODEL, LANES), lambda i, j: (layer, 0, 0)),
        ],
        out_specs=[
            pl.BlockSpec((tm, tn), lambda i, j: (i, j)),
            pl.BlockSpec((tm, LANES), lambda i, j: (i, 0)),
        ],
        out_shape=[
            jax.ShapeDtypeStruct((rows, D_MAIN), F32),
            jax.ShapeDtypeStruct((rows, LANES), F32),
        ],
        scratch_shapes=[pltpu.VMEM((tm, D_MODEL), BF16)],
        compiler_params=pltpu.CompilerParams(
            dimension_semantics=("arbitrary", "arbitrary"), vmem_limit_bytes=VMEM_LIMIT),
        name="in_proj",
    )(x, norm_g3, mod_arr, mod_arr, w_main, w_tail)


def _outproj_kernel(y_ref, x_ref, gate_ref, w_ref, fg_ref, o_ref, *, final):
    acc = _dot(y_ref[...].astype(BF16), w_ref[...])
    xn = x_ref[...] + gate_ref[...] * acc
    if final:
        xn = xn * lax.rsqrt(jnp.mean(xn * xn, axis=-1, keepdims=True) + EPS) * fg_ref[...]
    o_ref[...] = xn


def _outproj_call(layer, y, x, mod_arr, gate_spec, w_out, final_g, tm):
    rows = x.shape[0]
    return pl.pallas_call(
        functools.partial(_outproj_kernel, final=(layer == DEPTH - 1)),
        grid=(rows // tm,),
        in_specs=[
            pl.BlockSpec((tm, D_MODEL), lambda i: (i, 0)),
            pl.BlockSpec((tm, D_MODEL), lambda i: (i, 0)),
            gate_spec,
            pl.BlockSpec((None, D_MODEL, D_MODEL), lambda i: (layer, 0, 0)),
            pl.BlockSpec((1, D_MODEL), lambda i: (0, 0)),
        ],
        out_specs=pl.BlockSpec((tm, D_MODEL), lambda i: (i, 0)),
        out_shape=jax.ShapeDtypeStruct((rows, D_MODEL), F32),
        compiler_params=pltpu.CompilerParams(
            dimension_semantics=("arbitrary",), vmem_limit_bytes=VMEM_LIMIT),
        name="out_proj",
    )(y, x, mod_arr, w_out, final_g)


def _gate_terms(tail, alog_row, dtb_row):
    beta = jax.nn.sigmoid(tail)
    g = -jnp.exp(alog_row) * _softplus(tail + dtb_row)
    return beta, g


def _bdot(a, b):
    return lax.dot_general(a, b, (((2,), (1,)), ((0,), (0,))), preferred_element_type=F32)


def _bdot_nt(a, b):
    return lax.dot_general(a, b, (((2,), (2,)), ((0,), (0,))), preferred_element_type=F32)


def _bdot_tn(a, b):
    return lax.dot_general(a, b, (((1,), (1,)), ((0,), (0,))), preferred_element_type=F32)


def _split_bf16(a):
    hi = a.astype(BF16)
    lo = (a - hi.astype(F32)).astype(BF16)
    return hi, lo


def _unit_lower_inverse(lmat, eye):
    heads, c, _ = lmat.shape
    lane_lt_c = lax.broadcasted_iota(jnp.int32, (heads, c, 2 * c), 2) < c
    zeros_top = jnp.zeros((heads, c, 2 * c), BF16)
    w = jnp.concatenate([jnp.broadcast_to(eye, lmat.shape), -lmat], axis=2)
    for _ in range(6):
        w_hi, w_lo = _split_bf16(w)
        z_hi = jnp.concatenate([zeros_top, w_hi], axis=1)
        z_lo = jnp.concatenate([zeros_top, w_lo], axis=1)
        prod = _bdot(w_hi, z_hi) + _bdot(w_lo, z_hi) + _bdot(w_hi, z_lo)
        w = jnp.where(lane_lt_c, w, 0.0) + prod
    return w[:, :, :c]


def _mixer_prompt_kernel(z_ref, zt_ref, wa_ref, wq_ref, alog_ref, dtb_ref, on_ref,
                         y_ref, ca_ref, cq_ref, ss_ref, buf_a, buf_q, s_scr):
    c_idx = pl.program_id(1)
    C = CHUNK

    @pl.when(c_idx == 0)
    def _():
        buf_a[0:HALO, :] = jnp.zeros((HALO, D_CONV), F32)
        buf_q[0:HALO, :] = jnp.zeros((HALO, 3 * D_GDN), F32)
        s_scr[...] = jnp.zeros(s_scr.shape, F32)

    buf_a[HALO:HALO + C, :] = z_ref[:, OFF_CA:OFF_CA + D_CONV] * z_ref[:, OFF_HA:OFF_HA + D_CONV]
    conv = (buf_a[HALO - 2:HALO - 2 + C, :] * wa_ref[0:1, :]
            + buf_a[HALO - 1:HALO - 1 + C, :] * wa_ref[1:2, :]
            + buf_a[HALO:HALO + C, :] * wa_ref[2:3, :])
    ya = z_ref[:, OFF_BA:OFF_BA + D_CONV] * conv * _silu(z_ref[:, OFF_GA:OFF_GA + D_CONV])
    y_ref[:, 0:D_CONV] = ya.astype(y_ref.dtype)
    ca_ref[...] = buf_a[HALO + C - 2:HALO + C, :]
    buf_a[0:HALO, :] = buf_a[C:C + HALO, :]

    buf_q[HALO:HALO + C, :] = z_ref[:, OFF_QKV:OFF_QKV + 3 * D_GDN]
    cq_ref[...] = buf_q[HALO + C - 3:HALO + C, :]

    beta_all, g_all = _gate_terms(zt_ref[...], alog_ref[...], dtb_ref[...])
    row = lax.broadcasted_iota(jnp.int32, (C, C), 0)
    col = lax.broadcasted_iota(jnp.int32, (C, C), 1)
    causal = (row >= col)[None]
    strict = (row > col)[None]
    eye = (row == col).astype(F32)[None]
    gc_all = _dot_hi((row >= col).astype(F32), g_all)
    gc_t = gc_all.T
    eg_all = jnp.exp(gc_all)
    gc_last = gc_all[C - 1:C, :]
    kt_all = jnp.exp(gc_last - gc_all)
    gl_all = jnp.exp(gc_last)

    def per_head(fn):
        return jnp.stack([fn(h) for h in range(HEADS)], axis=0)

    def conv_qkv(off):
        acc = buf_q[HALO - 3:HALO - 3 + C, off:off + HEAD_DIM] * wq_ref[0:1, off:off + HEAD_DIM]
        for j in range(1, 4):
            acc = acc + (buf_q[HALO - 3 + j:HALO - 3 + j + C, off:off + HEAD_DIM]
                         * wq_ref[j:j + 1, off:off + HEAD_DIM])
        return _silu(acc)

    def lane_bcast(arr, lane0):
        return per_head(lambda h: jnp.broadcast_to(arr[:, lane0 + h:lane0 + h + 1], (C, HEAD_DIM)))

    q = per_head(lambda h: conv_qkv(h * HEAD_DIM))
    k = per_head(lambda h: conv_qkv(D_GDN + h * HEAD_DIM))
    v = per_head(lambda h: conv_qkv(2 * D_GDN + h * HEAD_DIM))
    q = q * (lax.rsqrt(jnp.sum(q * q, axis=-1, keepdims=True) + EPS) * (HEAD_DIM ** -0.5))
    k = k * lax.rsqrt(jnp.sum(k * k, axis=-1, keepdims=True) + EPS)
    beta = lane_bcast(beta_all, 0)
    eg = lane_bcast(eg_all, HEADS)
    k_decay = lane_bcast(kt_all, HEADS)
    gcol = per_head(lambda h: jnp.broadcast_to(gc_all[:, HEADS + h:HEADS + h + 1], (C, C)))
    grow = per_head(lambda h: gc_t[HEADS + h:HEADS + h + 1, :])
    s_decay = per_head(lambda h: gl_all[:, HEADS + h:HEADS + h + 1])

    dec_causal = jnp.exp(jnp.where(causal, gcol - grow, -jnp.inf))
    dec_strict = jnp.where(strict, dec_causal, 0.0)
    kb = k * beta
    k16 = k.astype(BF16)
    lmat = _bdot_nt(kb.astype(BF16), k16) * dec_strict
    attn = _bdot_nt(q.astype(BF16), k16) * dec_causal
    tmat = _unit_lower_inverse(lmat, eye)
    rhs = jnp.concatenate([v * beta, kb * eg], axis=2).astype(BF16)
    uw = _bdot(tmat.astype(BF16), rhs)
    s_old = s_scr[...]
    lhs = jnp.concatenate([uw[:, :, HEAD_DIM:], q * eg], axis=1).astype(BF16)
    ws = _bdot(lhs, s_old.astype(BF16))
    u = uw[:, :, :HEAD_DIM] - ws[:, :C]
    u16 = u.astype(BF16)
    o = ws[:, C:] + _bdot(attn.astype(BF16), u16)
    s_scr[...] = s_old * s_decay + _bdot_tn((k * k_decay).astype(BF16), u16)
    o = o * lax.rsqrt(jnp.mean(o * o, axis=-1, keepdims=True) + EPS) * on_ref[...]
    for h in range(HEADS):
        gb = z_ref[:, OFF_GB + h * HEAD_DIM:OFF_GB + (h + 1) * HEAD_DIM]
        y_ref[:, D_CONV + h * HEAD_DIM:D_CONV + (h + 1) * HEAD_DIM] = (o[h] * _silu(gb)).astype(y_ref.dtype)

    buf_q[0:HALO, :] = buf_q[C:C + HALO, :]

    @pl.when(c_idx == pl.num_programs(1) - 1)
    def _():
        ss_ref[...] = s_scr[...]


def _mixer_prompt_call(layer, z, zt, conv_a_w, conv_qkv_w, alog_rows, dtb_rows, onorm3, batch, seq):
    nc = seq // CHUNK
    lsel = lambda b, c: (layer, 0, 0)
    return pl.pallas_call(
        _mixer_prompt_kernel,
        grid=(batch, nc),
        in_specs=[
            pl.BlockSpec((CHUNK, D_MAIN), lambda b, c: (b * nc + c, 0)),
            pl.BlockSpec((CHUNK, LANES), lambda b, c: (b * nc + c, 0)),
            pl.BlockSpec((None, 3, D_CONV), lsel),
            pl.BlockSpec((None, 4, 3 * D_GDN), lsel),
            pl.BlockSpec((None, 1, LANES), lsel),
            pl.BlockSpec((None, 1, LANES), lsel),
            pl.BlockSpec((None, 1, HEAD_DIM), lsel),
        ],
        out_specs=[
            pl.BlockSpec((CHUNK, D_MODEL), lambda b, c: (b * nc + c, 0)),
            pl.BlockSpec((None, 2, D_CONV), lambda b, c: (b, 0, 0)),
            pl.BlockSpec((None, 3, 3 * D_GDN), lambda b, c: (b, 0, 0)),
            pl.BlockSpec((None, HEADS, HEAD_DIM, HEAD_DIM), lambda b, c: (b, 0, 0, 0)),
        ],
        out_shape=[
            jax.ShapeDtypeStruct((batch * seq, D_MODEL), BF16),
            jax.ShapeDtypeStruct((batch, 2, D_CONV), F32),
            jax.ShapeDtypeStruct((batch, 3, 3 * D_GDN), F32),
            jax.ShapeDtypeStruct((batch, HEADS, HEAD_DIM, HEAD_DIM), F32),
        ],
        scratch_shapes=[
            pltpu.VMEM((HALO + CHUNK, D_CONV), F32),
            pltpu.VMEM((HALO + CHUNK, 3 * D_GDN), F32),
            pltpu.VMEM((HEADS, HEAD_DIM, HEAD_DIM), F32),
        ],
        compiler_params=pltpu.CompilerParams(
            dimension_semantics=("arbitrary", "arbitrary"), vmem_limit_bytes=VMEM_LIMIT),
        name="mixer_prompt",
    )(z, zt, conv_a_w, conv_qkv_w, alog_rows, dtb_rows, onorm3)


def _mixer_sample_kernel(z_ref, zt_ref, sa_ref, sq_ref, s_ref, wa_ref, wq_ref, alog_ref, dtb_ref,
                         on_ref, y_ref, na_ref, nq_ref, ns_ref):
    nb = z_ref.shape[0]

    ch = z_ref[:, OFF_CA:OFF_CA + D_CONV] * z_ref[:, OFF_HA:OFF_HA + D_CONV]
    prev0 = sa_ref[:, 0:D_CONV]
    prev1 = sa_ref[:, D_CONV:2 * D_CONV]
    conv = prev0 * wa_ref[0:1, :] + prev1 * wa_ref[1:2, :] + ch * wa_ref[2:3, :]
    y_ref[:, 0:D_CONV] = (z_ref[:, OFF_BA:OFF_BA + D_CONV] * conv
                          * _silu(z_ref[:, OFF_GA:OFF_GA + D_CONV]))
    na_ref[:, 0:D_CONV] = prev1
    na_ref[:, D_CONV:2 * D_CONV] = ch

    w3 = 3 * D_GDN
    nq_ref[:, 0:w3] = sq_ref[:, w3:2 * w3]
    nq_ref[:, w3:2 * w3] = sq_ref[:, 2 * w3:3 * w3]
    nq_ref[:, 2 * w3:3 * w3] = z_ref[:, OFF_QKV:OFF_QKV + w3]

    beta_all, g_all = _gate_terms(zt_ref[...], alog_ref[...], dtb_ref[...])
    eg_all = jnp.exp(g_all)
    rowid = lax.broadcasted_iota(jnp.int32, (nb, HEAD_DIM), 0)

    def conv_qkv(off):
        sl = slice(off, off + HEAD_DIM)
        acc = z_ref[:, OFF_QKV + off:OFF_QKV + off + HEAD_DIM] * wq_ref[3:4, sl]
        for j in range(3):
            acc = acc + sq_ref[:, j * w3 + off:j * w3 + off + HEAD_DIM] * wq_ref[j:j + 1, sl]
        return _silu(acc)

    for h in range(HEADS):
        q = conv_qkv(h * HEAD_DIM)
        k = conv_qkv(D_GDN + h * HEAD_DIM)
        v = conv_qkv(2 * D_GDN + h * HEAD_DIM)
        q = q * (lax.rsqrt(jnp.sum(q * q, axis=-1, keepdims=True) + EPS) * (HEAD_DIM ** -0.5))
        k = k * lax.rsqrt(jnp.sum(k * k, axis=-1, keepdims=True) + EPS)
        beta = beta_all[:, h:h + 1]
        eg = eg_all[:, HEADS + h:HEADS + h + 1]
        kq = jnp.concatenate([k, q], axis=0)
        ks = jnp.zeros((nb, HEAD_DIM), F32)
        qs = jnp.zeros((nb, HEAD_DIM), F32)
        for j in range(nb):
            r = _dot(kq, s_ref[j, h])
            ks = jnp.where(rowid == j, r[:nb], ks)
            qs = jnp.where(rowid == j, r[nb:], qs)
        u = beta * (v - eg * ks)
        o = eg * qs + jnp.sum(q * k, axis=-1, keepdims=True) * u
        for j in range(nb):
            outer = _dot_tn(jnp.where(rowid == j, k, 0.0), u)
            ns_ref[j, h] = s_ref[j, h] * eg[j:j + 1, :] + outer
        o = o * lax.rsqrt(jnp.mean(o * o, axis=-1, keepdims=True) + EPS) * on_ref[...]
        gb = z_ref[:, OFF_GB + h * HEAD_DIM:OFF_GB + (h + 1) * HEAD_DIM]
        y_ref[:, D_CONV + h * HEAD_DIM:D_CONV + (h + 1) * HEAD_DIM] = o * _silu(gb)


def _mixer_sample_chained_kernel(prev_ref, *refs):
    del prev_ref
    _mixer_sample_kernel(*refs)


def _mixer_sample_call(layer, z, zt, sa, sq, state_ssm, conv_a_w, conv_qkv_w, alog_rows, dtb_rows,
                       onorm3, ssm_stack):
    nseq = z.shape[0]
    nb = 8
    lsel = lambda i: (layer, 0, 0)
    chained = ssm_stack is not None
    return pl.pallas_call(
        _mixer_sample_chained_kernel if chained else _mixer_sample_kernel,
        grid=(nseq // nb,),
        input_output_aliases={0: 3} if chained else {},
        in_specs=([pl.BlockSpec(memory_space=pl.ANY)] if chained else []) + [
            pl.BlockSpec((nb, D_MAIN), lambda i: (i, 0)),
            pl.BlockSpec((nb, LANES), lambda i: (i, 0)),
            pl.BlockSpec((nb, 2 * D_CONV), lambda i: (i, 0)),
            pl.BlockSpec((nb, 9 * D_GDN), lambda i: (i, 0)),
            pl.BlockSpec((None, nb, HEADS, HEAD_DIM, HEAD_DIM), lambda i: (layer, i, 0, 0, 0)),
            pl.BlockSpec((None, 3, D_CONV), lsel),
            pl.BlockSpec((None, 4, 3 * D_GDN), lsel),
            pl.BlockSpec((None, 1, LANES), lsel),
            pl.BlockSpec((None, 1, LANES), lsel),
            pl.BlockSpec((None, 1, HEAD_DIM), lsel),
        ],
        out_specs=[
            pl.BlockSpec((nb, D_MODEL), lambda i: (i, 0)),
            pl.BlockSpec((nb, 2 * D_CONV), lambda i: (i, 0)),
            pl.BlockSpec((nb, 9 * D_GDN), lambda i: (i, 0)),
            pl.BlockSpec((None, nb, HEADS, HEAD_DIM, HEAD_DIM), lambda i: (layer, i, 0, 0, 0)),
        ],
        out_shape=[
            jax.ShapeDtypeStruct((nseq, D_MODEL), F32),
            jax.ShapeDtypeStruct((nseq, 2 * D_CONV), F32),
            jax.ShapeDtypeStruct((nseq, 9 * D_GDN), F32),
            jax.ShapeDtypeStruct((DEPTH, nseq, HEADS, HEAD_DIM, HEAD_DIM), F32),
        ],
        compiler_params=pltpu.CompilerParams(
            dimension_semantics=("arbitrary",), vmem_limit_bytes=VMEM_LIMIT),
        name="mixer_sample",
    )(*([ssm_stack] if chained else []), z, zt, sa, sq, state_ssm, conv_a_w, conv_qkv_w,
      alog_rows, dtb_rows, onorm3)


def kernel(x_prompt, x_sample, state_conv_a, state_conv_qkv, state_ssm, c_prompt, c_sample, norm_g, w_ada, b_ada, w_in, conv_a_w, conv_qkv_w, a_log, dt_bias, o_norm_g, w_out, final_norm_g):
    batch, seq, _ = x_prompt.shape
    nseq = x_sample.shape[0]
    assert x_sample.shape[1] == 1 and seq % CHUNK == 0

    w_main = w_in.astype(BF16)
    w_tail = jnp.pad(w_in[:, :, D_MAIN:], ((0, 0), (0, 0), (0, LANES - 2 * HEADS))).astype(BF16)
    w_out16 = w_out.astype(BF16)
    norm_g3 = norm_g.reshape(DEPTH, 1, D_MODEL)
    onorm3 = o_norm_g.reshape(DEPTH, 1, HEAD_DIM)
    final_g = final_norm_g.reshape(1, D_MODEL)
    pad_heads = ((0, 0), (HEADS, LANES - 2 * HEADS))
    alog_rows = jnp.pad(a_log, pad_heads).reshape(DEPTH, 1, LANES)
    dtb_rows = jnp.pad(dt_bias, pad_heads).reshape(DEPTH, 1, LANES)

    n_cond = nseq + batch
    n_cond_pad = -(-n_cond // 8) * 8
    c_all = jnp.concatenate(
        [c_sample, c_prompt, jnp.zeros((n_cond_pad - n_cond, D_MODEL), F32)], axis=0)
    mod = _mod_call(c_all, w_ada, b_ada)
    mod4 = mod.reshape(DEPTH, n_cond_pad, 1, 3 * D_MODEL)

    xp = x_prompt.reshape(batch * seq, D_MODEL)
    xs = x_sample.reshape(nseq, D_MODEL)
    tm_in = 1024
    tm_out = 512
    outs = {k: [] for k in ("ca_p", "cq_p", "ss_p", "ca_s", "cq_s")}
    ssm_stack = None

    for layer in range(DEPTH):
        def pmod(sec, per_batch, layer=layer):
            return pl.BlockSpec((None, None, 1, D_MODEL),
                                lambda i, *_: (layer, nseq + i // per_batch, 0, sec))
        z, zt = _inproj_call(layer, xp, norm_g3, mod4,
                             (pmod(1, seq // tm_in), pmod(0, seq // tm_in)), w_main, w_tail, tm_in)
        y, ca, cq, ss = _mixer_prompt_call(layer, z, zt, conv_a_w, conv_qkv_w, alog_rows, dtb_rows,
                                           onorm3, batch, seq)
        xp = _outproj_call(layer, y, xp, mod4, pmod(2, seq // tm_out), w_out16, final_g, tm_out)
        outs["ca_p"].append(ca)
        outs["cq_p"].append(cq)
        outs["ss_p"].append(ss)

        def smod(sec, layer=layer):
            return pl.BlockSpec((None, nseq, D_MODEL), lambda i, *_: (layer, 0, sec))
        z, zt = _inproj_call(layer, xs, norm_g3, mod, (smod(1), smod(0)), w_main, w_tail, nseq)
        sa = state_conv_a[layer].reshape(nseq, 2 * D_CONV)
        sq = state_conv_qkv[layer].reshape(nseq, 9 * D_GDN)
        y, na, nq, ssm_stack = _mixer_sample_call(layer, z, zt, sa, sq, state_ssm, conv_a_w,
                                                  conv_qkv_w, alog_rows, dtb_rows, onorm3, ssm_stack)
        xs = _outproj_call(layer, y, xs, mod, smod(2), w_out16, final_g, nseq)
        outs["ca_s"].append(na.reshape(nseq, 2, D_CONV))
        outs["cq_s"].append(nq.reshape(nseq, 3, 3 * D_GDN))

    return (xp.reshape(batch, seq, D_MODEL), xs.reshape(nseq, 1, D_MODEL),
            jnp.stack(outs["ca_p"]), jnp.stack(outs["cq_p"]), jnp.stack(outs["ss_p"]),
            jnp.stack(outs["ca_s"]), jnp.stack(outs["cq_s"]), ssm_stack)
```

```python
import functools
import math

import jax
import jax.numpy as jnp
from jax import lax
from jax.experimental import pallas as pl
from jax.experimental.pallas import tpu as pltpu

F32 = jnp.float32
BF16 = jnp.bfloat16

D_MODEL = 2048
DEPTH = 4
D_CONV = 1024
D_GDN = 1024
HEADS = 8
HEAD_DIM = 128
CHUNK = 64
EPS = 1e-6
D_MAIN = 8192
OFF_BA, OFF_CA, OFF_HA, OFF_GA, OFF_QKV, OFF_GB = 0, 1024, 2048, 3072, 4096, 7168
LANES = 128
HALO = 8
VMEM_LIMIT = 56 * 1024 * 1024
PROMPT_STREAMS = 2


def _silu(x):
    hx = 0.5 * x
    return hx + hx * jnp.tanh(hx)


def _softplus(x):
    return jnp.maximum(x, 0.0) + jnp.log1p(jnp.exp(-jnp.abs(x)))


def _dot(a, b):
    return jnp.dot(a, b, preferred_element_type=F32)


def _dot_nt(a, b):
    return lax.dot_general(a, b, (((1,), (1,)), ((), ())), preferred_element_type=F32)


def _dot_tn(a, b):
    return lax.dot_general(a, b, (((0,), (0,)), ((), ())), preferred_element_type=F32)


def _dot_hi(a, b):
    return jnp.dot(a, b, preferred_element_type=F32, precision=lax.Precision.HIGHEST)


def _mod_kernel(c_ref, w_ref, b_ref, o_ref):
    s = _silu(c_ref[...]).astype(BF16)
    o_ref[...] = _dot(s, w_ref[...].astype(BF16)) + b_ref[...]


def _mod_call(c_all, w_ada, b_ada):
    rows = c_all.shape[0]
    tn = 1024
    return pl.pallas_call(
        _mod_kernel,
        grid=(DEPTH, 3 * D_MODEL // tn),
        in_specs=[
            pl.BlockSpec((rows, D_MODEL), lambda l, j: (0, 0)),
            pl.BlockSpec((None, D_MODEL, tn), lambda l, j: (l, 0, j)),
            pl.BlockSpec((None, 1, tn), lambda l, j: (l, 0, j)),
        ],
        out_specs=pl.BlockSpec((None, rows, tn), lambda l, j: (l, 0, j)),
        out_shape=jax.ShapeDtypeStruct((DEPTH, rows, 3 * D_MODEL), F32),
        compiler_params=pltpu.CompilerParams(
            dimension_semantics=("arbitrary", "arbitrary"), vmem_limit_bytes=VMEM_LIMIT),
        name="adaln_mod",
    )(c_all, w_ada, b_ada.reshape(DEPTH, 1, 3 * D_MODEL))


def _inproj_kernel(x_ref, g_ref, sc_ref, sh_ref, w_ref, wt_ref, z_ref, zt_ref, h_scr):
    @pl.when(pl.program_id(1) == 0)
    def _():
        x = x_ref[...]
        y = x * lax.rsqrt(jnp.mean(x * x, axis=-1, keepdims=True) + EPS) * g_ref[...]
        h = (y * (1.0 + sc_ref[...]) + sh_ref[...]).astype(BF16)
        h_scr[...] = h
        zt_ref[...] = _dot(h, wt_ref[...])

    z_ref[...] = _dot(h_scr[...], w_ref[...])


def _inproj_call(layer, x, norm_g3, mod_arr, mod_specs, w_main, w_tail, tm):
    rows = x.shape[0]
    tn = 1024
    sc_spec, sh_spec = mod_specs
    return pl.pallas_call(
        _inproj_kernel,
        grid=(rows // tm, D_MAIN // tn),
        in_specs=[
            pl.BlockSpec((tm, D_MODEL), lambda i, j: (i, 0)),
            pl.BlockSpec((None, 1, D_MODEL), lambda i, j: (layer, 0, 0)),
            sc_spec,
            sh_spec,
            pl.BlockSpec((None, D_MODEL, tn), lambda i, j: (layer, 0, j)),
            pl.BlockSpec((None, D_MODEL, LANES), lambda i, j: (layer, 0, 0)),
        ],
        out_specs=[
            pl.BlockSpec((tm, tn), lambda i, j: (i, j)),
            pl.BlockSpec((tm, LANES), lambda i, j: (i, 0)),
        ],
        out_shape=[
            jax.ShapeDtypeStruct((rows, D_MAIN), F32),
            jax.ShapeDtypeStruct((rows, LANES), F32),
        ],
        scratch_shapes=[pltpu.VMEM((tm, D_MODEL), BF16)],
        compiler_params=pltpu.CompilerParams(
            dimension_semantics=("arbitrary", "arbitrary"), vmem_limit_bytes=VMEM_LIMIT),
        name="in_proj",
    )(x, norm_g3, mod_arr, mod_arr, w_main, w_tail)


def _outproj_kernel(y_ref, x_ref, gate_ref, w_ref, fg_ref, o_ref, *, final):
    acc = _dot(y_ref[...].astype(BF16), w_ref[...])
    xn = x_ref[...] + gate_ref[...] * acc
    if final:
        xn = xn * lax.rsqrt(jnp.mean(xn * xn, axis=-1, keepdims=True) + EPS) * fg_ref[...]
    o_ref[...] = xn


def _outproj_call(layer, y, x, mod_arr, gate_spec, w_out, final_g, tm):
    rows = x.shape[0]
    return pl.pallas_call(
        functools.partial(_outproj_kernel, final=(layer == DEPTH - 1)),
        grid=(rows // tm,),
        in_specs=[
            pl.BlockSpec((tm, D_MODEL), lambda i: (i, 0)),
            pl.BlockSpec((tm, D_MODEL), lambda i: (i, 0)),
            gate_spec,
            pl.BlockSpec((None, D_MODEL, D_MODEL), lambda i: (layer, 0, 0)),
            pl.BlockSpec((1, D_MODEL), lambda i: (0, 0)),
        ],
        out_specs=pl.BlockSpec((tm, D_MODEL), lambda i: (i, 0)),
        out_shape=jax.ShapeDtypeStruct((rows, D_MODEL), F32),
        compiler_params=pltpu.CompilerParams(
            dimension_semantics=("arbitrary",), vmem_limit_bytes=VMEM_LIMIT),
        name="out_proj",
    )(y, x, mod_arr, w_out, final_g)


def _gate_terms(tail, alog_row, dtb_row):
    beta = jax.nn.sigmoid(tail)
    g = -jnp.exp(alog_row) * _softplus(tail + dtb_row)
    return beta, g


def _bdot(a, b):
    return lax.dot_general(a, b, (((2,), (1,)), ((0,), (0,))), preferred_element_type=F32)


def _bdot_nt(a, b):
    return lax.dot_general(a, b, (((2,), (2,)), ((0,), (0,))), preferred_element_type=F32)


def _bdot_tn(a, b):
    return lax.dot_general(a, b, (((1,), (1,)), ((0,), (0,))), preferred_element_type=F32)


def _split_bf16(a):
    hi = a.astype(BF16)
    lo = (a - hi.astype(F32)).astype(BF16)
    return hi, lo


def _unit_lower_inverse(lmat, eye):
    heads, c, _ = lmat.shape
    eye_b = jnp.broadcast_to(eye, lmat.shape)
    pass_x = jnp.concatenate([eye_b, jnp.zeros_like(eye_b)], axis=2).astype(BF16)
    zeros_top = jnp.zeros((heads, c, 2 * c), BF16)
    w = jnp.concatenate([eye_b, -lmat], axis=2)
    for _ in range(6):
        w_hi, w_lo = _split_bf16(w)
        lhs = jnp.concatenate([w_hi, w_lo, w_hi], axis=2)
        rhs = jnp.concatenate([pass_x, w_hi, pass_x, w_hi, zeros_top, w_lo], axis=1)
        w = _bdot(lhs, rhs)
    return w[:, :, :c]


def _mixer_prompt_kernel(z_ref, zt_ref, wa_ref, wq_ref, alog_ref, dtb_ref, on_ref,
                         y_ref, ca_ref, cq_ref, ss_ref, buf_a, buf_q, s_scr):
    c_idx = pl.program_id(1)
    C = CHUNK
    ns = z_ref.shape[0]

    @pl.when(c_idx == 0)
    def _():
        buf_a[:, 0:HALO, :] = jnp.zeros((ns, HALO, D_CONV), F32)
        buf_q[:, 0:HALO, :] = jnp.zeros((ns, HALO, 3 * D_GDN), F32)
        s_scr[...] = jnp.zeros(s_scr.shape, F32)

    row = lax.broadcasted_iota(jnp.int32, (C, C), 0)
    col = lax.broadcasted_iota(jnp.int32, (C, C), 1)
    causal = (row >= col)[None]
    strict = (row > col)[None]
    eye = (row == col).astype(F32)[None]
    tri = (row >= col).astype(F32)

    gc_all, gc_t, beta_all = [], [], []
    for s in range(ns):
        buf_a[s, HALO:HALO + C, :] = (z_ref[s, :, OFF_CA:OFF_CA + D_CONV]
                                      * z_ref[s, :, OFF_HA:OFF_HA + D_CONV])
        conv = (buf_a[s, HALO - 2:HALO - 2 + C, :] * wa_ref[0:1, :]
                + buf_a[s, HALO - 1:HALO - 1 + C, :] * wa_ref[1:2, :]
                + buf_a[s, HALO:HALO + C, :] * wa_ref[2:3, :])
        ya = (z_ref[s, :, OFF_BA:OFF_BA + D_CONV] * conv
              * _silu(z_ref[s, :, OFF_GA:OFF_GA + D_CONV]))
        y_ref[s, :, 0:D_CONV] = ya.astype(y_ref.dtype)
        ca_ref[s] = buf_a[s, HALO + C - 2:HALO + C, :]
        buf_a[s, 0:HALO, :] = buf_a[s, C:C + HALO, :]

        buf_q[s, HALO:HALO + C, :] = z_ref[s, :, OFF_QKV:OFF_QKV + 3 * D_GDN]
        cq_ref[s] = buf_q[s, HALO + C - 3:HALO + C, :]
        beta_s, g_s = _gate_terms(zt_ref[s], alog_ref[...], dtb_ref[...])
        gc_s = _dot_hi(tri, g_s)
        beta_all.append(beta_s)
        gc_all.append(gc_s)
        gc_t.append(gc_s.T)

    def per_head(fn):
        return jnp.stack([fn(s, h) for s in range(ns) for h in range(HEADS)], axis=0)

    def conv_qkv(s, off):
        acc = (buf_q[s, HALO - 3:HALO - 3 + C, off:off + HEAD_DIM]
               * wq_ref[0:1, off:off + HEAD_DIM])
        for j in range(1, 4):
            acc = acc + (buf_q[s, HALO - 3 + j:HALO - 3 + j + C, off:off + HEAD_DIM]
                         * wq_ref[j:j + 1, off:off + HEAD_DIM])
        return _silu(acc)

    def lane_bcast(arrs, lane0):
        return per_head(lambda s, h: jnp.broadcast_to(
            arrs[s][:, lane0 + h:lane0 + h + 1], (C, HEAD_DIM)))

    q = per_head(lambda s, h: conv_qkv(s, h * HEAD_DIM))
    k = per_head(lambda s, h: conv_qkv(s, D_GDN + h * HEAD_DIM))
    v = per_head(lambda s, h: conv_qkv(s, 2 * D_GDN + h * HEAD_DIM))
    q = q * (lax.rsqrt(jnp.sum(q * q, axis=-1, keepdims=True) + EPS) * (HEAD_DIM ** -0.5))
    k = k * lax.rsqrt(jnp.sum(k * k, axis=-1, keepdims=True) + EPS)
    beta = lane_bcast(beta_all, 0)
    gcol = lane_bcast(gc_all, HEADS)
    grow = per_head(lambda s, h: gc_t[s][HEADS + h:HEADS + h + 1, :])
    glast = gcol[:, C - 1:C, :]
    eg = jnp.exp(gcol)
    k_decay = jnp.exp(glast - gcol)
    s_decay = jnp.exp(glast)

    dec_causal = jnp.exp(jnp.where(causal, gcol[:, :, :C] - grow, -jnp.inf))
    dec_strict = jnp.where(strict, dec_causal, 0.0)
    kb = k * beta
    k16 = k.astype(BF16)
    lmat = _bdot_nt(kb.astype(BF16), k16) * dec_strict
    attn = _bdot_nt(q.astype(BF16), k16) * dec_causal
    tmat = _unit_lower_inverse(lmat, eye)
    rhs = jnp.concatenate([v * beta, kb * eg], axis=2).astype(BF16)
    uw = _bdot(tmat.astype(BF16), rhs)
    s_old = s_scr[...]
    lhs = jnp.concatenate([uw[:, :, HEAD_DIM:], q * eg], axis=1).astype(BF16)
    ws = _bdot(lhs, s_old.astype(BF16))
    u = uw[:, :, :HEAD_DIM] - ws[:, :C]
    u16 = u.astype(BF16)
    o = ws[:, C:] + _bdot(attn.astype(BF16), u16)
    s_scr[...] = s_old * s_decay + _bdot_tn((k * k_decay).astype(BF16), u16)
    o = o * lax.rsqrt(jnp.mean(o * o, axis=-1, keepdims=True) + EPS) * on_ref[...]
    for s in range(ns):
        for h in range(HEADS):
            gb = z_ref[s, :, OFF_GB + h * HEAD_DIM:OFF_GB + (h + 1) * HEAD_DIM]
            y_ref[s, :, D_CONV + h * HEAD_DIM:D_CONV + (h + 1) * HEAD_DIM] = (
                o[s * HEADS + h] * _silu(gb)).astype(y_ref.dtype)
        buf_q[s, 0:HALO, :] = buf_q[s, C:C + HALO, :]

    @pl.when(c_idx == pl.num_programs(1) - 1)
    def _():
        ss_ref[...] = s_scr[...].reshape(ss_ref.shape)


def _mixer_prompt_call(layer, z, zt, conv_a_w, conv_qkv_w, alog_rows, dtb_rows, onorm3, batch, seq):
    nc = seq // CHUNK
    ns = PROMPT_STREAMS
    lsel = lambda g, c: (layer, 0, 0)
    y, ca, cq, ss = pl.pallas_call(
        _mixer_prompt_kernel,
        grid=(batch // ns, nc),
        in_specs=[
            pl.BlockSpec((ns, CHUNK, D_MAIN), lambda g, c: (g, c, 0)),
            pl.BlockSpec((ns, CHUNK, LANES), lambda g, c: (g, c, 0)),
            pl.BlockSpec((None, 3, D_CONV), lsel),
            pl.BlockSpec((None, 4, 3 * D_GDN), lsel),
            pl.BlockSpec((None, 1, LANES), lsel),
            pl.BlockSpec((None, 1, LANES), lsel),
            pl.BlockSpec((None, 1, HEAD_DIM), lsel),
        ],
        out_specs=[
            pl.BlockSpec((ns, CHUNK, D_MODEL), lambda g, c: (g, c, 0)),
            pl.BlockSpec((ns, 2, D_CONV), lambda g, c: (g, 0, 0)),
            pl.BlockSpec((ns, 3, 3 * D_GDN), lambda g, c: (g, 0, 0)),
            pl.BlockSpec((ns, HEADS, HEAD_DIM, HEAD_DIM), lambda g, c: (g, 0, 0, 0)),
        ],
        out_shape=[
            jax.ShapeDtypeStruct((batch, seq, D_MODEL), BF16),
            jax.ShapeDtypeStruct((batch, 2, D_CONV), F32),
            jax.ShapeDtypeStruct((batch, 3, 3 * D_GDN), F32),
            jax.ShapeDtypeStruct((batch, HEADS, HEAD_DIM, HEAD_DIM), F32),
        ],
        scratch_shapes=[
            pltpu.VMEM((ns, HALO + CHUNK, D_CONV), F32),
            pltpu.VMEM((ns, HALO + CHUNK, 3 * D_GDN), F32),
            pltpu.VMEM((ns * HEADS, HEAD_DIM, HEAD_DIM), F32),
        ],
        compiler_params=pltpu.CompilerParams(
            dimension_semantics=("arbitrary", "arbitrary"), vmem_limit_bytes=VMEM_LIMIT),
        name="mixer_prompt",
    )(z.reshape(batch, seq, D_MAIN), zt.reshape(batch, seq, LANES), conv_a_w, conv_qkv_w,
      alog_rows, dtb_rows, onorm3)
    return y.reshape(batch * seq, D_MODEL), ca, cq, ss


def _mixer_sample_kernel(z_ref, zt_ref, sa_ref, sq_ref, s_ref, wa_ref, wq_ref, alog_ref, dtb_ref,
                         on_ref, y_ref, na_ref, nq_ref, ns_ref):
    nb = z_ref.shape[0]

    ch = z_ref[:, OFF_CA:OFF_CA + D_CONV] * z_ref[:, OFF_HA:OFF_HA + D_CONV]
    prev0 = sa_ref[:, 0:D_CONV]
    prev1 = sa_ref[:, D_CONV:2 * D_CONV]
    conv = prev0 * wa_ref[0:1, :] + prev1 * wa_ref[1:2, :] + ch * wa_ref[2:3, :]
    y_ref[:, 0:D_CONV] = (z_ref[:, OFF_BA:OFF_BA + D_CONV] * conv
                          * _silu(z_ref[:, OFF_GA:OFF_GA + D_CONV]))
    na_ref[:, 0:D_CONV] = prev1
    na_ref[:, D_CONV:2 * D_CONV] = ch

    w3 = 3 * D_GDN
    nq_ref[:, 0:w3] = sq_ref[:, w3:2 * w3]
    nq_ref[:, w3:2 * w3] = sq_ref[:, 2 * w3:3 * w3]
    nq_ref[:, 2 * w3:3 * w3] = z_ref[:, OFF_QKV:OFF_QKV + w3]

    beta_all, g_all = _gate_terms(zt_ref[...], alog_ref[...], dtb_ref[...])
    eg_all = jnp.exp(g_all)
    rowid = lax.broadcasted_iota(jnp.int32, (nb, HEAD_DIM), 0)

    def conv_qkv(off):
        sl = slice(off, off + HEAD_DIM)
        acc = z_ref[:, OFF_QKV + off:OFF_QKV + off + HEAD_DIM] * wq_ref[3:4, sl]
        for j in range(3):
            acc = acc + sq_ref[:, j * w3 + off:j * w3 + off + HEAD_DIM] * wq_ref[j:j + 1, sl]
        return _silu(acc)

    for h in range(HEADS):
        q = conv_qkv(h * HEAD_DIM)
        k = conv_qkv(D_GDN + h * HEAD_DIM)
        v = conv_qkv(2 * D_GDN + h * HEAD_DIM)
        q = q * (lax.rsqrt(jnp.sum(q * q, axis=-1, keepdims=True) + EPS) * (HEAD_DIM ** -0.5))
        k = k * lax.rsqrt(jnp.sum(k * k, axis=-1, keepdims=True) + EPS)
        beta = beta_all[:, h:h + 1]
        eg = eg_all[:, HEADS + h:HEADS + h + 1]
        kq = jnp.concatenate([k, q], axis=0)
        ks = jnp.zeros((nb, HEAD_DIM), F32)
        qs = jnp.zeros((nb, HEAD_DIM), F32)
        for j in range(nb):
            r = _dot(kq, s_ref[j, h])
            ks = jnp.where(rowid == j, r[:nb], ks)
            qs = jnp.where(rowid == j, r[nb:], qs)
        u = beta * (v - eg * ks)
        o = eg * qs + jnp.sum(q * k, axis=-1, keepdims=True) * u
        for j in range(nb):
            outer = _dot_tn(jnp.where(rowid == j, k, 0.0), u)
            ns_ref[j, h] = s_ref[j, h] * eg[j:j + 1, :] + outer
        o = o * lax.rsqrt(jnp.mean(o * o, axis=-1, keepdims=True) + EPS) * on_ref[...]
        gb = z_ref[:, OFF_GB + h * HEAD_DIM:OFF_GB + (h + 1) * HEAD_DIM]
        y_ref[:, D_CONV + h * HEAD_DIM:D_CONV + (h + 1) * HEAD_DIM] = o * _silu(gb)


def _mixer_sample_chained_kernel(prev_ref, *refs):
    del prev_ref
    _mixer_sample_kernel(*refs)


def _mixer_sample_call(layer, z, zt, sa, sq, state_ssm, conv_a_w, conv_qkv_w, alog_rows, dtb_rows,
                       onorm3, ssm_stack):
    nseq = z.shape[0]
    nb = 8
    lsel = lambda i: (layer, 0, 0)
    chained = ssm_stack is not None
    return pl.pallas_call(
        _mixer_sample_chained_kernel if chained else _mixer_sample_kernel,
        grid=(nseq // nb,),
        input_output_aliases={0: 3} if chained else {},
        in_specs=([pl.BlockSpec(memory_space=pl.ANY)] if chained else []) + [
            pl.BlockSpec((nb, D_MAIN), lambda i: (i, 0)),
            pl.BlockSpec((nb, LANES), lambda i: (i, 0)),
            pl.BlockSpec((nb, 2 * D_CONV), lambda i: (i, 0)),
            pl.BlockSpec((nb, 9 * D_GDN), lambda i: (i, 0)),
            pl.BlockSpec((None, nb, HEADS, HEAD_DIM, HEAD_DIM), lambda i: (layer, i, 0, 0, 0)),
            pl.BlockSpec((None, 3, D_CONV), lsel),
            pl.BlockSpec((None, 4, 3 * D_GDN), lsel),
            pl.BlockSpec((None, 1, LANES), lsel),
            pl.BlockSpec((None, 1, LANES), lsel),
            pl.BlockSpec((None, 1, HEAD_DIM), lsel),
        ],
        out_specs=[
            pl.BlockSpec((nb, D_MODEL), lambda i: (i, 0)),
            pl.BlockSpec((nb, 2 * D_CONV), lambda i: (i, 0)),
            pl.BlockSpec((nb, 9 * D_GDN), lambda i: (i, 0)),
            pl.BlockSpec((None, nb, HEADS, HEAD_DIM, HEAD_DIM), lambda i: (layer, i, 0, 0, 0)),
        ],
        out_shape=[
            jax.ShapeDtypeStruct((nseq, D_MODEL), F32),
            jax.ShapeDtypeStruct((nseq, 2 * D_CONV), F32),
            jax.ShapeDtypeStruct((nseq, 9 * D_GDN), F32),
            jax.ShapeDtypeStruct((DEPTH, nseq, HEADS, HEAD_DIM, HEAD_DIM), F32),
        ],
        compiler_params=pltpu.CompilerParams(
            dimension_semantics=("arbitrary",), vmem_limit_bytes=VMEM_LIMIT),
        name="mixer_sample",
    )(*([ssm_stack] if chained else []), z, zt, sa, sq, state_ssm, conv_a_w, conv_qkv_w,
      alog_rows, dtb_rows, onorm3)


def kernel(x_prompt, x_sample, state_conv_a, state_conv_qkv, state_ssm, c_prompt, c_sample, norm_g, w_ada, b_ada, w_in, conv_a_w, conv_qkv_w, a_log, dt_bias, o_norm_g, w_out, final_norm_g):
    batch, seq, _ = x_prompt.shape
    nseq = x_sample.shape[0]
    assert x_sample.shape[1] == 1 and seq % CHUNK == 0

    w_main = w_in.astype(BF16)
    w_tail = jnp.pad(w_in[:, :, D_MAIN:], ((0, 0), (0, 0), (0, LANES - 2 * HEADS))).astype(BF16)
    w_out16 = w_out.astype(BF16)
    norm_g3 = norm_g.reshape(DEPTH, 1, D_MODEL)
    onorm3 = o_norm_g.reshape(DEPTH, 1, HEAD_DIM)
    final_g = final_norm_g.reshape(1, D_MODEL)
    pad_heads = ((0, 0), (HEADS, LANES - 2 * HEADS))
    alog_rows = jnp.pad(a_log, pad_heads).reshape(DEPTH, 1, LANES)
    dtb_rows = jnp.pad(dt_bias, pad_heads).reshape(DEPTH, 1, LANES)

    n_cond = nseq + batch
    n_cond_pad = -(-n_cond // 8) * 8
    c_all = jnp.concatenate(
        [c_sample, c_prompt, jnp.zeros((n_cond_pad - n_cond, D_MODEL), F32)], axis=0)
    mod = _mod_call(c_all, w_ada, b_ada)
    mod4 = mod.reshape(DEPTH, n_cond_pad, 1, 3 * D_MODEL)

    xp = x_prompt.reshape(batch * seq, D_MODEL)
    xs = x_sample.reshape(nseq, D_MODEL)
    tm_in = 1024
    tm_out = 512
    outs = {k: [] for k in ("ca_p", "cq_p", "ss_p", "ca_s", "cq_s")}
    ssm_stack = None

    for layer in range(DEPTH):
        def pmod(sec, per_batch, layer=layer):
            return pl.BlockSpec((None, None, 1, D_MODEL),
                                lambda i, *_: (layer, nseq + i // per_batch, 0, sec))
        z, zt = _inproj_call(layer, xp, norm_g3, mod4,
                             (pmod(1, seq // tm_in), pmod(0, seq // tm_in)), w_main, w_tail, tm_in)
        y, ca, cq, ss = _mixer_prompt_call(layer, z, zt, conv_a_w, conv_qkv_w, alog_rows, dtb_rows,
                                           onorm3, batch, seq)
        xp = _outproj_call(layer, y, xp, mod4, pmod(2, seq // tm_out), w_out16, final_g, tm_out)
        outs["ca_p"].append(ca)
        outs["cq_p"].append(cq)
        outs["ss_p"].append(ss)

        def smod(sec, layer=layer):
            return pl.BlockSpec((None, nseq, D_MODEL), lambda i, *_: (layer, 0, sec))
        z, zt = _inproj_call(layer, xs, norm_g3, mod, (smod(1), smod(0)), w_main, w_tail, nseq)
        sa = state_conv_a[layer].reshape(nseq, 2 * D_CONV)
        sq = state_conv_qkv[layer].reshape(nseq, 9 * D_GDN)
        y, na, nq, ssm_stack = _mixer_sample_call(layer, z, zt, sa, sq, state_ssm, conv_a_w,
                                                  conv_qkv_w, alog_rows, dtb_rows, onorm3, ssm_stack)
        xs = _outproj_call(layer, y, xs, mod, smod(2), w_out16, final_g, nseq)
        outs["ca_s"].append(na.reshape(nseq, 2, D_CONV))
        outs["cq_s"].append(nq.reshape(nseq, 3, 3 * D_GDN))

    return (xp.reshape(batch, seq, D_MODEL), xs.reshape(nseq, 1, D_MODEL),
            jnp.stack(outs["ca_p"]), jnp.stack(outs["cq_p"]), jnp.stack(outs["ss_p"]),
            jnp.stack(outs["ca_s"]), jnp.stack(outs["cq_s"]), ssm_stack)
```

```python
import functools
from typing import NamedTuple

import jax
import jax.numpy as jnp
from jax import lax
from jax.experimental import pallas as pl
from jax.experimental.pallas import tpu as pltpu

F32 = jnp.float32
BF16 = jnp.bfloat16

D_MODEL = 2048
DEPTH = 4
D_CONV = 1024
D_GDN = 1024
HEADS = 8
HEAD_DIM = 128
CHUNK = 64
EPS = 1e-6
D_MAIN = 8192
OFF_BA, OFF_CA, OFF_HA, OFF_GA, OFF_QKV, OFF_GB = 0, 1024, 2048, 3072, 4096, 7168
LANES = 128
HALO = 8
VMEM_LIMIT = 56 * 1024 * 1024
PROMPT_STREAMS = 2
PREP_TASKS_PER_GAP = (8, 5, 5, 5, 4, 3, 2, 2, 0, 0)


def _silu(x):
    hx = 0.5 * x
    return hx + hx * jnp.tanh(hx)


def _softplus(x):
    return jnp.maximum(x, 0.0) + jnp.log1p(jnp.exp(-jnp.abs(x)))


def _dot(a, b):
    return jnp.dot(a, b, preferred_element_type=F32)


def _dot_tn(a, b):
    return lax.dot_general(a, b, (((0,), (0,)), ((), ())), preferred_element_type=F32)


def _dot_hi(a, b):
    return jnp.dot(a, b, preferred_element_type=F32, precision=lax.Precision.HIGHEST)


def _mod_kernel(c_ref, w_ref, b_ref, o_ref):
    s = _silu(c_ref[...]).astype(BF16)
    o_ref[...] = _dot(s, w_ref[...].astype(BF16)) + b_ref[...]


def _mod_call(c_all, w_ada, b_ada):
    rows = c_all.shape[0]
    tn = 1024
    return pl.pallas_call(
        _mod_kernel,
        grid=(DEPTH, 3 * D_MODEL // tn),
        in_specs=[
            pl.BlockSpec((rows, D_MODEL), lambda l, j: (0, 0)),
            pl.BlockSpec((None, D_MODEL, tn), lambda l, j: (l, 0, j)),
            pl.BlockSpec((None, 1, tn), lambda l, j: (l, 0, j)),
        ],
        out_specs=pl.BlockSpec((None, rows, tn), lambda l, j: (l, 0, j)),
        out_shape=jax.ShapeDtypeStruct((DEPTH, rows, 3 * D_MODEL), F32),
        compiler_params=pltpu.CompilerParams(
            dimension_semantics=("arbitrary", "arbitrary"), vmem_limit_bytes=VMEM_LIMIT),
        name="adaln_mod",
    )(c_all, w_ada, b_ada.reshape(DEPTH, 1, 3 * D_MODEL))


def _inproj_kernel(x_ref, g_ref, sc_ref, sh_ref, w_ref, wt_ref, z_ref, zt_ref, h_scr):
    @pl.when(pl.program_id(1) == 0)
    def _():
        x = x_ref[...]
        y = x * lax.rsqrt(jnp.mean(x * x, axis=-1, keepdims=True) + EPS) * g_ref[...]
        h = (y * (1.0 + sc_ref[...]) + sh_ref[...]).astype(BF16)
        h_scr[...] = h
        zt_ref[...] = _dot(h, wt_ref[...])

    z_ref[...] = _dot(h_scr[...], w_ref[...])


def _inproj_call(layer, x, norm_g3, mod_arr, mod_specs, w_main, w_tail, tm):
    rows = x.shape[0]
    tn = 1024
    sc_spec, sh_spec = mod_specs
    return pl.pallas_call(
        _inproj_kernel,
        grid=(rows // tm, D_MAIN // tn),
        in_specs=[
            pl.BlockSpec((tm, D_MODEL), lambda i, j: (i, 0)),
            pl.BlockSpec((None, 1, D_MODEL), lambda i, j: (layer, 0, 0)),
            sc_spec,
            sh_spec,
            pl.BlockSpec((None, D_MODEL, tn), lambda i, j: (layer, 0, j)),
            pl.BlockSpec((None, D_MODEL, LANES), lambda i, j: (layer, 0, 0)),
        ],
        out_specs=[
            pl.BlockSpec((tm, tn), lambda i, j: (i, j)),
            pl.BlockSpec((tm, LANES), lambda i, j: (i, 0)),
        ],
        out_shape=[
            jax.ShapeDtypeStruct((rows, D_MAIN), F32),
            jax.ShapeDtypeStruct((rows, LANES), F32),
        ],
        scratch_shapes=[pltpu.VMEM((tm, D_MODEL), BF16)],
        compiler_params=pltpu.CompilerParams(
            dimension_semantics=("arbitrary", "arbitrary"), vmem_limit_bytes=VMEM_LIMIT),
        name="in_proj",
    )(x, norm_g3, mod_arr, mod_arr, w_main, w_tail)


def _outproj_kernel(ya_ref, yb_ref, x_ref, gate_ref, wa_ref, wb_ref, fg_ref, o_ref, *, final):
    acc = (_dot(ya_ref[...].astype(BF16), wa_ref[...])
           + _dot(yb_ref[...].astype(BF16), wb_ref[...]))
    xn = x_ref[...] + gate_ref[...] * acc
    if final:
        xn = xn * lax.rsqrt(jnp.mean(xn * xn, axis=-1, keepdims=True) + EPS) * fg_ref[...]
    o_ref[...] = xn


def _outproj_call(layer, ya, yb, y_specs, x, mod_arr, gate_spec, w_out, final_g, tm):
    rows = x.shape[0]
    half = D_MODEL // 2
    return pl.pallas_call(
        functools.partial(_outproj_kernel, final=(layer == DEPTH - 1)),
        grid=(rows // tm,),
        in_specs=[
            *y_specs,
            pl.BlockSpec((tm, D_MODEL), lambda i: (i, 0)),
            gate_spec,
            pl.BlockSpec((None, half, D_MODEL), lambda i: (layer, 0, 0)),
            pl.BlockSpec((None, half, D_MODEL), lambda i: (layer, 1, 0)),
            pl.BlockSpec((1, D_MODEL), lambda i: (0, 0)),
        ],
        out_specs=pl.BlockSpec((tm, D_MODEL), lambda i: (i, 0)),
        out_shape=jax.ShapeDtypeStruct((rows, D_MODEL), F32),
        compiler_params=pltpu.CompilerParams(
            dimension_semantics=("arbitrary",), vmem_limit_bytes=VMEM_LIMIT),
        name="out_proj",
    )(ya, yb, x, mod_arr, w_out, w_out, final_g)


def _gate_terms(tail, alog_row, dtb_row):
    beta = jax.nn.sigmoid(tail)
    g = -jnp.exp(alog_row) * _softplus(tail + dtb_row)
    return beta, g


def _bdot(a, b):
    return lax.dot_general(a, b, (((2,), (1,)), ((0,), (0,))), preferred_element_type=F32)


def _bdot_nt(a, b):
    return lax.dot_general(a, b, (((2,), (2,)), ((0,), (0,))), preferred_element_type=F32)


def _bdot_tn(a, b):
    return lax.dot_general(a, b, (((1,), (1,)), ((0,), (0,))), preferred_element_type=F32)


def _split_bf16(a):
    hi = a.astype(BF16)
    lo = (a - hi.astype(F32)).astype(BF16)
    return hi, lo


def _unit_lower_inverse_levels(lmat, eye):
    heads, c, _ = lmat.shape
    eye_b = jnp.broadcast_to(eye, lmat.shape)
    pass_x = jnp.concatenate([eye_b, jnp.zeros_like(eye_b)], axis=2).astype(BF16)
    zeros_top = jnp.zeros((heads, c, 2 * c), BF16)
    w = jnp.concatenate([eye_b, -lmat], axis=2)
    for _ in range(6):
        w_hi, w_lo = _split_bf16(w)
        lhs = jnp.concatenate([w_hi, w_lo, w_hi], axis=2)
        rhs = jnp.concatenate([pass_x, w_hi, pass_x, w_hi, zeros_top, w_lo], axis=1)
        w = _bdot(lhs, rhs)
        yield w[:, :, :c], w


class _ChunkOperands(NamedTuple):
    k16: object
    kb16: object
    q16: object
    rhs16: object
    qeg16: object
    kt16: object
    dec: object
    sdec: object
    gate: object


def _operand_scratch(ns):
    nh = ns * HEADS
    return _ChunkOperands(
        k16=pltpu.VMEM((nh, CHUNK, HEAD_DIM), BF16),
        kb16=pltpu.VMEM((nh, CHUNK, HEAD_DIM), BF16),
        q16=pltpu.VMEM((nh, CHUNK, HEAD_DIM), BF16),
        rhs16=pltpu.VMEM((nh, CHUNK, 2 * HEAD_DIM), BF16),
        qeg16=pltpu.VMEM((nh, CHUNK, HEAD_DIM), BF16),
        kt16=pltpu.VMEM((nh, CHUNK, HEAD_DIM), BF16),
        dec=pltpu.VMEM((nh, CHUNK, CHUNK), F32),
        sdec=pltpu.VMEM((nh, 1, HEAD_DIM), F32),
        gate=pltpu.VMEM((ns, CHUNK, D_GDN), F32),
    )


def _prep_tasks(z_ref, zt_ref, wa_ref, wq_ref, alog_ref, dtb_ref, ya_ref, ca_ref, cq_ref,
                buf_a, buf_q, out):
    C = CHUNK
    ns = z_ref.shape[0]
    row = lax.broadcasted_iota(jnp.int32, (C, C), 0)
    col = lax.broadcasted_iota(jnp.int32, (C, C), 1)
    causal = row >= col
    tri = causal.astype(F32)
    sub = lax.broadcasted_iota(jnp.int32, (C // 8, 8, HEAD_DIM), 1)
    never = pl.program_id(1) < 0

    def pinned(row_vec, token):
        return row_vec if token is None else jnp.where(never, token, row_vec)

    def shifted(x3, d):
        r = pltpu.roll(x3, d, axis=1)
        return jnp.where(sub < d, r[:-1], r[1:]).reshape(C, x3.shape[-1])

    def haloed(buf, s, sl):
        x = buf[s, :, sl]
        return x, x.reshape((HALO + C) // 8, 8, HEAD_DIM)

    gates = {}

    def gate_task(s):
        def run(token):
            del token
            beta_s, g_s = _gate_terms(zt_ref[s], alog_ref[...], dtb_ref[...])
            gc_s = _dot_hi(tri, g_s)
            gates[s] = (beta_s, gc_s, gc_s.T)
            buf_a[s, HALO:HALO + C, :] = (z_ref[s, :, OFF_CA:OFF_CA + D_CONV]
                                          * z_ref[s, :, OFF_HA:OFF_HA + D_CONV])
            ca_ref[s] = buf_a[s, HALO + C - 2:HALO + C, :]
            buf_q[s, HALO:HALO + C, :] = z_ref[s, :, OFF_QKV:OFF_QKV + 3 * D_GDN]
            cq_ref[s] = buf_q[s, HALO + C - 3:HALO + C, :]
        return run

    def conv_a_task(s, off):
        def run(token):
            sl = slice(off, off + HEAD_DIM)
            xa, xa3 = haloed(buf_a, s, sl)
            conv = (shifted(xa3, 2) * pinned(wa_ref[0:1, sl], token)
                    + shifted(xa3, 1) * pinned(wa_ref[1:2, sl], token)
                    + xa[HALO:] * pinned(wa_ref[2:3, sl], token))
            ya = (z_ref[s, :, OFF_BA + off:OFF_BA + off + HEAD_DIM] * conv
                  * _silu(z_ref[s, :, OFF_GA + off:OFF_GA + off + HEAD_DIM]))
            ya_ref[s, :, sl] = ya.astype(ya_ref.dtype)
            buf_a[s, 0:HALO, sl] = xa[C:]
        return run

    def conv_qkv(s, off, token):
        sl = slice(off, off + HEAD_DIM)
        x, x3 = haloed(buf_q, s, sl)
        acc = x[HALO:] * pinned(wq_ref[3:4, sl], token)
        for d in range(1, 4):
            acc = acc + shifted(x3, d) * pinned(wq_ref[3 - d:4 - d, sl], token)
        buf_q[s, 0:HALO, sl] = x[C:]
        return _silu(acc)

    def head_task(s, h):
        def run(token):
            n = s * HEADS + h
            beta_s, gc_s, gc_t = gates[s]
            sl = slice(h * HEAD_DIM, (h + 1) * HEAD_DIM)
            q = conv_qkv(s, h * HEAD_DIM, token)
            k = conv_qkv(s, D_GDN + h * HEAD_DIM, token)
            v = conv_qkv(s, 2 * D_GDN + h * HEAD_DIM, token)
            q = q * (lax.rsqrt(jnp.sum(q * q, axis=-1, keepdims=True) + EPS) * (HEAD_DIM ** -0.5))
            k = k * lax.rsqrt(jnp.sum(k * k, axis=-1, keepdims=True) + EPS)
            beta = jnp.broadcast_to(beta_s[:, h:h + 1], (C, HEAD_DIM))
            gcol = jnp.broadcast_to(gc_s[:, HEADS + h:HEADS + h + 1], (C, HEAD_DIM))
            grow = gc_t[HEADS + h:HEADS + h + 1, :]
            glast = gcol[C - 1:C, :]
            eg = jnp.exp(gcol)
            kb = k * beta
            out.k16[n] = k.astype(BF16)
            out.kb16[n] = kb.astype(BF16)
            out.q16[n] = q.astype(BF16)
            out.rhs16[n] = jnp.concatenate([v * beta, kb * eg], axis=1).astype(BF16)
            out.qeg16[n] = (q * eg).astype(BF16)
            out.kt16[n] = (k * jnp.exp(glast - gcol)).astype(BF16)
            out.dec[n] = jnp.exp(jnp.where(causal, gcol[:, :C] - grow, -jnp.inf))
            out.sdec[n] = jnp.exp(glast)
            out.gate[s, :, sl] = _silu(z_ref[s, :, OFF_GB + h * HEAD_DIM:OFF_GB + (h + 1) * HEAD_DIM])
        return run

    tasks = [gate_task(s) for s in range(ns)]
    for s in range(ns):
        for h in range(HEADS):
            tasks.append(head_task(s, h))
            tasks.append(conv_a_task(s, h * HEAD_DIM))
    return tasks


def _scan_stages(ops, on_ref, yb_ref, s_scr):
    C = CHUNK
    ns = yb_ref.shape[0]
    row = lax.broadcasted_iota(jnp.int32, (C, C), 0)
    col = lax.broadcasted_iota(jnp.int32, (C, C), 1)
    strict = (row > col)[None]
    eye = (row == col).astype(F32)[None]

    k16 = ops.k16[...]
    dec_causal = ops.dec[...]
    lmat = _bdot_nt(ops.kb16[...], k16) * jnp.where(strict, dec_causal, 0.0)
    attn = (_bdot_nt(ops.q16[...], k16) * dec_causal).astype(BF16)
    nh = k16.shape[0]
    token_of = lambda a: a[nh - 1, C - 1:C, :HEAD_DIM]
    yield None
    for tmat, w_level in _unit_lower_inverse_levels(lmat, eye):
        yield token_of(w_level)
    uw = _bdot(tmat.astype(BF16), ops.rhs16[...])
    yield token_of(uw)
    s_old = s_scr[...]
    lhs = jnp.concatenate([uw[:, :, HEAD_DIM:].astype(BF16), ops.qeg16[...]], axis=1)
    ws = _bdot(lhs, s_old.astype(BF16))
    yield token_of(ws)
    u = uw[:, :, :HEAD_DIM] - ws[:, :C]
    u16 = u.astype(BF16)
    o = ws[:, C:] + _bdot(attn, u16)
    s_scr[...] = s_old * ops.sdec[...] + _bdot_tn(ops.kt16[...], u16)
    yield token_of(o)
    o = o * lax.rsqrt(jnp.mean(o * o, axis=-1, keepdims=True) + EPS) * on_ref[...]
    for s in range(ns):
        for h in range(HEADS):
            sl = slice(h * HEAD_DIM, (h + 1) * HEAD_DIM)
            yb_ref[s, :, sl] = (o[s * HEADS + h] * ops.gate[s, :, sl]).astype(yb_ref.dtype)


def _interleave(stages, tasks, per_gap):
    todo = list(tasks)
    token = None
    for n, token in zip(list(per_gap) + [0] * 64, stages):
        for task in todo[:n]:
            task(token)
        todo = todo[n:]
    for task in todo:
        task(token)


def _mixer_prompt_kernel(z_ref, zt_ref, wa_ref, wq_ref, alog_ref, dtb_ref, on_ref,
                         ya_ref, yb_ref, ca_ref, cq_ref, ss_ref, buf_a, buf_q, s_scr, *operand_refs):
    t = pl.program_id(1)
    ns = z_ref.shape[0]
    n_fields = len(_ChunkOperands._fields)
    sets = (_ChunkOperands(*operand_refs[:n_fields]), _ChunkOperands(*operand_refs[n_fields:]))

    @pl.when(t == 0)
    def _():
        buf_a[:, 0:HALO, :] = jnp.zeros((ns, HALO, D_CONV), F32)
        buf_q[:, 0:HALO, :] = jnp.zeros((ns, HALO, 3 * D_GDN), F32)
        s_scr[...] = jnp.zeros(s_scr.shape, F32)
        for ref in sets[1]:
            ref[...] = jnp.zeros(ref.shape, ref.dtype)

    for parity in range(2):
        @pl.when(t % 2 == parity)
        def _(parity=parity):
            _interleave(
                _scan_stages(sets[1 - parity], on_ref, yb_ref, s_scr),
                _prep_tasks(z_ref, zt_ref, wa_ref, wq_ref, alog_ref, dtb_ref, ya_ref, ca_ref,
                            cq_ref, buf_a, buf_q, sets[parity]),
                PREP_TASKS_PER_GAP)

    @pl.when(t == pl.num_programs(1) - 1)
    def _():
        ss_ref[...] = s_scr[...].reshape(ss_ref.shape)


def _mixer_prompt_call(layer, z, zt, conv_a_w, conv_qkv_w, alog_rows, dtb_rows, onorm3, batch, seq):
    nc = seq // CHUNK
    ns = PROMPT_STREAMS
    lsel = lambda g, t: (layer, 0, 0)
    cur = lambda g, t: (g, jnp.minimum(t, nc - 1), 0)
    prev = lambda g, t: (g, jnp.maximum(t - 1, 0), 0)
    scratch = _operand_scratch(ns)
    return pl.pallas_call(
        _mixer_prompt_kernel,
        grid=(batch // ns, nc + 1),
        in_specs=[
            pl.BlockSpec((ns, CHUNK, D_MAIN), cur),
            pl.BlockSpec((ns, CHUNK, LANES), cur),
            pl.BlockSpec((None, 3, D_CONV), lsel),
            pl.BlockSpec((None, 4, 3 * D_GDN), lsel),
            pl.BlockSpec((None, 1, LANES), lsel),
            pl.BlockSpec((None, 1, LANES), lsel),
            pl.BlockSpec((None, 1, HEAD_DIM), lsel),
        ],
        out_specs=[
            pl.BlockSpec((ns, CHUNK, D_CONV), lambda g, t: (g, t, 0)),
            pl.BlockSpec((ns, CHUNK, D_GDN), prev),
            pl.BlockSpec((ns, 2, D_CONV), lambda g, t: (g, 0, 0)),
            pl.BlockSpec((ns, 3, 3 * D_GDN), lambda g, t: (g, 0, 0)),
            pl.BlockSpec((ns, HEADS, HEAD_DIM, HEAD_DIM), lambda g, t: (g, 0, 0, 0)),
        ],
        out_shape=[
            jax.ShapeDtypeStruct((batch, seq + CHUNK, D_CONV), BF16),
            jax.ShapeDtypeStruct((batch, seq, D_GDN), BF16),
            jax.ShapeDtypeStruct((batch, 2, D_CONV), F32),
            jax.ShapeDtypeStruct((batch, 3, 3 * D_GDN), F32),
            jax.ShapeDtypeStruct((batch, HEADS, HEAD_DIM, HEAD_DIM), F32),
        ],
        scratch_shapes=[
            pltpu.VMEM((ns, HALO + CHUNK, D_CONV), F32),
            pltpu.VMEM((ns, HALO + CHUNK, 3 * D_GDN), F32),
            pltpu.VMEM((ns * HEADS, HEAD_DIM, HEAD_DIM), F32),
            *scratch, *scratch,
        ],
        compiler_params=pltpu.CompilerParams(
            dimension_semantics=("arbitrary", "arbitrary"), vmem_limit_bytes=VMEM_LIMIT),
        name="mixer_prompt",
    )(z.reshape(batch, seq, D_MAIN), zt.reshape(batch, seq, LANES), conv_a_w, conv_qkv_w,
      alog_rows, dtb_rows, onorm3)


def _mixer_sample_kernel(z_ref, zt_ref, sa_ref, sq_ref, s_ref, wa_ref, wq_ref, alog_ref, dtb_ref,
                         on_ref, y_ref, na_ref, nq_ref, ns_ref):
    nb = z_ref.shape[0]

    ch = z_ref[:, OFF_CA:OFF_CA + D_CONV] * z_ref[:, OFF_HA:OFF_HA + D_CONV]
    prev0 = sa_ref[:, 0:D_CONV]
    prev1 = sa_ref[:, D_CONV:2 * D_CONV]
    conv = prev0 * wa_ref[0:1, :] + prev1 * wa_ref[1:2, :] + ch * wa_ref[2:3, :]
    y_ref[:, 0:D_CONV] = (z_ref[:, OFF_BA:OFF_BA + D_CONV] * conv
                          * _silu(z_ref[:, OFF_GA:OFF_GA + D_CONV]))
    na_ref[:, 0:D_CONV] = prev1
    na_ref[:, D_CONV:2 * D_CONV] = ch

    w3 = 3 * D_GDN
    nq_ref[:, 0:w3] = sq_ref[:, w3:2 * w3]
    nq_ref[:, w3:2 * w3] = sq_ref[:, 2 * w3:3 * w3]
    nq_ref[:, 2 * w3:3 * w3] = z_ref[:, OFF_QKV:OFF_QKV + w3]

    beta_all, g_all = _gate_terms(zt_ref[...], alog_ref[...], dtb_ref[...])
    eg_all = jnp.exp(g_all)
    rowid = lax.broadcasted_iota(jnp.int32, (nb, HEAD_DIM), 0)

    def conv_qkv(off):
        sl = slice(off, off + HEAD_DIM)
        acc = z_ref[:, OFF_QKV + off:OFF_QKV + off + HEAD_DIM] * wq_ref[3:4, sl]
        for j in range(3):
            acc = acc + sq_ref[:, j * w3 + off:j * w3 + off + HEAD_DIM] * wq_ref[j:j + 1, sl]
        return _silu(acc)

    for h in range(HEADS):
        q = conv_qkv(h * HEAD_DIM)
        k = conv_qkv(D_GDN + h * HEAD_DIM)
        v = conv_qkv(2 * D_GDN + h * HEAD_DIM)
        q = q * (lax.rsqrt(jnp.sum(q * q, axis=-1, keepdims=True) + EPS) * (HEAD_DIM ** -0.5))
        k = k * lax.rsqrt(jnp.sum(k * k, axis=-1, keepdims=True) + EPS)
        beta = beta_all[:, h:h + 1]
        eg = eg_all[:, HEADS + h:HEADS + h + 1]
        kq = jnp.concatenate([k, q], axis=0)
        ks = jnp.zeros((nb, HEAD_DIM), F32)
        qs = jnp.zeros((nb, HEAD_DIM), F32)
        for j in range(nb):
            r = _dot(kq, s_ref[j, h])
            ks = jnp.where(rowid == j, r[:nb], ks)
            qs = jnp.where(rowid == j, r[nb:], qs)
        u = beta * (v - eg * ks)
        o = eg * qs + jnp.sum(q * k, axis=-1, keepdims=True) * u
        for j in range(nb):
            outer = _dot_tn(jnp.where(rowid == j, k, 0.0), u)
            ns_ref[j, h] = s_ref[j, h] * eg[j:j + 1, :] + outer
        o = o * lax.rsqrt(jnp.mean(o * o, axis=-1, keepdims=True) + EPS) * on_ref[...]
        gb = z_ref[:, OFF_GB + h * HEAD_DIM:OFF_GB + (h + 1) * HEAD_DIM]
        y_ref[:, D_CONV + h * HEAD_DIM:D_CONV + (h + 1) * HEAD_DIM] = o * _silu(gb)


def _mixer_sample_chained_kernel(prev_ref, *refs):
    del prev_ref
    _mixer_sample_kernel(*refs)


def _mixer_sample_call(layer, z, zt, sa, sq, state_ssm, conv_a_w, conv_qkv_w, alog_rows, dtb_rows,
                       onorm3, ssm_stack):
    nseq = z.shape[0]
    nb = 8
    lsel = lambda i: (layer, 0, 0)
    chained = ssm_stack is not None
    return pl.pallas_call(
        _mixer_sample_chained_kernel if chained else _mixer_sample_kernel,
        grid=(nseq // nb,),
        input_output_aliases={0: 3} if chained else {},
        in_specs=([pl.BlockSpec(memory_space=pl.ANY)] if chained else []) + [
            pl.BlockSpec((nb, D_MAIN), lambda i: (i, 0)),
            pl.BlockSpec((nb, LANES), lambda i: (i, 0)),
            pl.BlockSpec((nb, 2 * D_CONV), lambda i: (i, 0)),
            pl.BlockSpec((nb, 9 * D_GDN), lambda i: (i, 0)),
            pl.BlockSpec((None, nb, HEADS, HEAD_DIM, HEAD_DIM), lambda i: (layer, i, 0, 0, 0)),
            pl.BlockSpec((None, 3, D_CONV), lsel),
            pl.BlockSpec((None, 4, 3 * D_GDN), lsel),
            pl.BlockSpec((None, 1, LANES), lsel),
            pl.BlockSpec((None, 1, LANES), lsel),
            pl.BlockSpec((None, 1, HEAD_DIM), lsel),
        ],
        out_specs=[
            pl.BlockSpec((nb, D_MODEL), lambda i: (i, 0)),
            pl.BlockSpec((nb, 2 * D_CONV), lambda i: (i, 0)),
            pl.BlockSpec((nb, 9 * D_GDN), lambda i: (i, 0)),
            pl.BlockSpec((None, nb, HEADS, HEAD_DIM, HEAD_DIM), lambda i: (layer, i, 0, 0, 0)),
        ],
        out_shape=[
            jax.ShapeDtypeStruct((nseq, D_MODEL), F32),
            jax.ShapeDtypeStruct((nseq, 2 * D_CONV), F32),
            jax.ShapeDtypeStruct((nseq, 9 * D_GDN), F32),
            jax.ShapeDtypeStruct((DEPTH, nseq, HEADS, HEAD_DIM, HEAD_DIM), F32),
        ],
        compiler_params=pltpu.CompilerParams(
            dimension_semantics=("arbitrary",), vmem_limit_bytes=VMEM_LIMIT),
        name="mixer_sample",
    )(*([ssm_stack] if chained else []), z, zt, sa, sq, state_ssm, conv_a_w, conv_qkv_w,
      alog_rows, dtb_rows, onorm3)


def kernel(x_prompt, x_sample, state_conv_a, state_conv_qkv, state_ssm, c_prompt, c_sample, norm_g, w_ada, b_ada, w_in, conv_a_w, conv_qkv_w, a_log, dt_bias, o_norm_g, w_out, final_norm_g):
    batch, seq, _ = x_prompt.shape
    nseq = x_sample.shape[0]
    assert x_sample.shape[1] == 1 and seq % CHUNK == 0

    w_main = w_in.astype(BF16)
    w_tail = jnp.pad(w_in[:, :, D_MAIN:], ((0, 0), (0, 0), (0, LANES - 2 * HEADS))).astype(BF16)
    w_out16 = w_out.astype(BF16)
    norm_g3 = norm_g.reshape(DEPTH, 1, D_MODEL)
    onorm3 = o_norm_g.reshape(DEPTH, 1, HEAD_DIM)
    final_g = final_norm_g.reshape(1, D_MODEL)
    pad_heads = ((0, 0), (HEADS, LANES - 2 * HEADS))
    alog_rows = jnp.pad(a_log, pad_heads).reshape(DEPTH, 1, LANES)
    dtb_rows = jnp.pad(dt_bias, pad_heads).reshape(DEPTH, 1, LANES)

    n_cond = nseq + batch
    n_cond_pad = -(-n_cond // 8) * 8
    c_all = jnp.concatenate(
        [c_sample, c_prompt, jnp.zeros((n_cond_pad - n_cond, D_MODEL), F32)], axis=0)
    mod = _mod_call(c_all, w_ada, b_ada)
    mod4 = mod.reshape(DEPTH, n_cond_pad, 1, 3 * D_MODEL)

    xp = x_prompt.reshape(batch * seq, D_MODEL)
    xs = x_sample.reshape(nseq, D_MODEL)
    tm_in = 1024
    tm_out = 512
    outs = {k: [] for k in ("ca_p", "cq_p", "ss_p", "ca_s", "cq_s")}
    ssm_stack = None

    for layer in range(DEPTH):
        def pmod(sec, per_batch, layer=layer):
            return pl.BlockSpec((None, None, 1, D_MODEL),
                                lambda i, *_: (layer, nseq + i // per_batch, 0, sec))
        z, zt = _inproj_call(layer, xp, norm_g3, mod4,
                             (pmod(1, seq // tm_in), pmod(0, seq // tm_in)), w_main, w_tail, tm_in)
        ya, yb, ca, cq, ss = _mixer_prompt_call(layer, z, zt, conv_a_w, conv_qkv_w, alog_rows,
                                                dtb_rows, onorm3, batch, seq)
        per_seq = seq // tm_out
        y_specs = [pl.BlockSpec((None, tm_out, D_CONV), lambda i: (i // per_seq, i % per_seq, 0)),
                   pl.BlockSpec((None, tm_out, D_GDN), lambda i: (i // per_seq, i % per_seq, 0))]
        xp = _outproj_call(layer, ya, yb, y_specs, xp, mod4, pmod(2, per_seq), w_out16, final_g,
                           tm_out)
        outs["ca_p"].append(ca)
        outs["cq_p"].append(cq)
        outs["ss_p"].append(ss)

        def smod(sec, layer=layer):
            return pl.BlockSpec((None, nseq, D_MODEL), lambda i, *_: (layer, 0, sec))
        z, zt = _inproj_call(layer, xs, norm_g3, mod, (smod(1), smod(0)), w_main, w_tail, nseq)
        sa = state_conv_a[layer].reshape(nseq, 2 * D_CONV)
        sq = state_conv_qkv[layer].reshape(nseq, 9 * D_GDN)
        y, na, nq, ssm_stack = _mixer_sample_call(layer, z, zt, sa, sq, state_ssm, conv_a_w,
                                                  conv_qkv_w, alog_rows, dtb_rows, onorm3, ssm_stack)
        y_specs = [pl.BlockSpec((nseq, D_CONV), lambda i: (0, 0)),
                   pl.BlockSpec((nseq, D_GDN), lambda i: (0, 1))]
        xs = _outproj_call(layer, y, y, y_specs, xs, mod, smod(2), w_out16, final_g, nseq)
        outs["ca_s"].append(na.reshape(nseq, 2, D_CONV))
        outs["cq_s"].append(nq.reshape(nseq, 3, 3 * D_GDN))

    return (xp.reshape(batch, seq, D_MODEL), xs.reshape(nseq, 1, D_MODEL),
            jnp.stack(outs["ca_p"]), jnp.stack(outs["cq_p"]), jnp.stack(outs["ss_p"]),
            jnp.stack(outs["ca_s"]), jnp.stack(outs["cq_s"]), ssm_stack)
```

```python
import functools
from typing import NamedTuple

import jax
import jax.numpy as jnp
from jax import lax
from jax.experimental import pallas as pl
from jax.experimental.pallas import tpu as pltpu

F32 = jnp.float32
BF16 = jnp.bfloat16

D_MODEL = 2048
DEPTH = 4
D_CONV = 1024
D_GDN = 1024
HEADS = 8
HEAD_DIM = 128
CHUNK = 64
EPS = 1e-6
D_MAIN = 8192
OFF_BA, OFF_CA, OFF_HA, OFF_GA, OFF_QKV, OFF_GB = 0, 1024, 2048, 3072, 4096, 7168
LANES = 128
HALO = 8
VMEM_LIMIT = 56 * 1024 * 1024
PROMPT_STREAMS = 2
PREP_TASKS_PER_GAP = (8, 5, 5, 5, 4, 3, 2, 2, 0, 0)


def _silu(x):
    hx = 0.5 * x
    return hx + hx * jnp.tanh(hx)


def _softplus(x):
    return jnp.maximum(x, 0.0) + jnp.log1p(jnp.exp(-jnp.abs(x)))


def _dot(a, b):
    return jnp.dot(a, b, preferred_element_type=F32)


def _dot_tn(a, b):
    return lax.dot_general(a, b, (((0,), (0,)), ((), ())), preferred_element_type=F32)


def _dot_hi(a, b):
    return jnp.dot(a, b, preferred_element_type=F32, precision=lax.Precision.HIGHEST)


def _mod_kernel(c_ref, w_ref, b_ref, o_ref):
    s = _silu(c_ref[...]).astype(BF16)
    o_ref[...] = _dot(s, w_ref[...].astype(BF16)) + b_ref[...]


def _mod_call(c_all, w_ada, b_ada):
    rows = c_all.shape[0]
    tn = 1024
    return pl.pallas_call(
        _mod_kernel,
        grid=(DEPTH, 3 * D_MODEL // tn),
        in_specs=[
            pl.BlockSpec((rows, D_MODEL), lambda l, j: (0, 0)),
            pl.BlockSpec((None, D_MODEL, tn), lambda l, j: (l, 0, j)),
            pl.BlockSpec((None, 1, tn), lambda l, j: (l, 0, j)),
        ],
        out_specs=pl.BlockSpec((None, rows, tn), lambda l, j: (l, 0, j)),
        out_shape=jax.ShapeDtypeStruct((DEPTH, rows, 3 * D_MODEL), F32),
        compiler_params=pltpu.CompilerParams(
            dimension_semantics=("arbitrary", "arbitrary"), vmem_limit_bytes=VMEM_LIMIT),
        name="adaln_mod",
    )(c_all, w_ada, b_ada.reshape(DEPTH, 1, 3 * D_MODEL))


def _inproj_kernel(x_ref, g_ref, sc_ref, sh_ref, w_ref, wt_ref, z_ref, zt_ref, h_scr):
    @pl.when(pl.program_id(1) == 0)
    def _():
        x = x_ref[...]
        y = x * lax.rsqrt(jnp.mean(x * x, axis=-1, keepdims=True) + EPS) * g_ref[...]
        h = (y * (1.0 + sc_ref[...]) + sh_ref[...]).astype(BF16)
        h_scr[...] = h
        zt_ref[...] = _dot(h, wt_ref[...])

    z_ref[...] = _dot(h_scr[...], w_ref[...].astype(BF16))


def _inproj_call(layer, x, norm_g3, mod_arr, mod_specs, w_main, w_tail, tm):
    rows = x.shape[0]
    tn = 1024
    sc_spec, sh_spec = mod_specs
    return pl.pallas_call(
        _inproj_kernel,
        grid=(rows // tm, D_MAIN // tn),
        in_specs=[
            pl.BlockSpec((tm, D_MODEL), lambda i, j: (i, 0)),
            pl.BlockSpec((None, 1, D_MODEL), lambda i, j: (layer, 0, 0)),
            sc_spec,
            sh_spec,
            pl.BlockSpec((None, D_MODEL, tn), lambda i, j: (layer, 0, j)),
            pl.BlockSpec((None, D_MODEL, LANES), lambda i, j: (layer, 0, 0)),
        ],
        out_specs=[
            pl.BlockSpec((tm, tn), lambda i, j: (i, j)),
            pl.BlockSpec((tm, LANES), lambda i, j: (i, 0)),
        ],
        out_shape=[
            jax.ShapeDtypeStruct((rows, D_MAIN), F32),
            jax.ShapeDtypeStruct((rows, LANES), F32),
        ],
        scratch_shapes=[pltpu.VMEM((tm, D_MODEL), BF16)],
        compiler_params=pltpu.CompilerParams(
            dimension_semantics=("arbitrary", "arbitrary"), vmem_limit_bytes=VMEM_LIMIT),
        name="in_proj",
    )(x, norm_g3, mod_arr, mod_arr, w_main, w_tail)


def _outproj_kernel(ya_ref, yb_ref, x_ref, gate_ref, wa_ref, wb_ref, fg_ref, o_ref, *, final):
    acc = (_dot(ya_ref[...].astype(BF16), wa_ref[...])
           + _dot(yb_ref[...].astype(BF16), wb_ref[...]))
    xn = x_ref[...] + gate_ref[...] * acc
    if final:
        xn = xn * lax.rsqrt(jnp.mean(xn * xn, axis=-1, keepdims=True) + EPS) * fg_ref[...]
    o_ref[...] = xn


def _outproj_call(layer, ya, yb, y_specs, x, mod_arr, gate_spec, w_out, final_g, tm):
    rows = x.shape[0]
    half = D_MODEL // 2
    return pl.pallas_call(
        functools.partial(_outproj_kernel, final=(layer == DEPTH - 1)),
        grid=(rows // tm,),
        in_specs=[
            *y_specs,
            pl.BlockSpec((tm, D_MODEL), lambda i: (i, 0)),
            gate_spec,
            pl.BlockSpec((None, half, D_MODEL), lambda i: (layer, 0, 0)),
            pl.BlockSpec((None, half, D_MODEL), lambda i: (layer, 1, 0)),
            pl.BlockSpec((1, D_MODEL), lambda i: (0, 0)),
        ],
        out_specs=pl.BlockSpec((tm, D_MODEL), lambda i: (i, 0)),
        out_shape=jax.ShapeDtypeStruct((rows, D_MODEL), F32),
        compiler_params=pltpu.CompilerParams(
            dimension_semantics=("arbitrary",), vmem_limit_bytes=VMEM_LIMIT),
        name="out_proj",
    )(ya, yb, x, mod_arr, w_out, w_out, final_g)


def _gate_terms(tail, alog_row, dtb_row):
    beta = jax.nn.sigmoid(tail)
    g = -jnp.exp(alog_row) * _softplus(tail + dtb_row)
    return beta, g


def _bdot(a, b):
    return lax.dot_general(a, b, (((2,), (1,)), ((0,), (0,))), preferred_element_type=F32)


def _bdot_nt(a, b):
    return lax.dot_general(a, b, (((2,), (2,)), ((0,), (0,))), preferred_element_type=F32)


def _bdot_tn(a, b):
    return lax.dot_general(a, b, (((1,), (1,)), ((0,), (0,))), preferred_element_type=F32)


def _split_bf16(a):
    hi = a.astype(BF16)
    lo = (a - hi.astype(F32)).astype(BF16)
    return hi, lo


def _unit_lower_inverse_levels(lmat, eye):
    heads, c, _ = lmat.shape
    eye_b = jnp.broadcast_to(eye, lmat.shape)
    pass_x = jnp.concatenate([eye_b, jnp.zeros_like(eye_b)], axis=2).astype(BF16)
    zeros_top = jnp.zeros((heads, c, 2 * c), BF16)
    w = jnp.concatenate([eye_b, -lmat], axis=2)
    for _ in range(6):
        w_hi, w_lo = _split_bf16(w)
        lhs = jnp.concatenate([w_hi, w_lo, w_hi], axis=2)
        rhs = jnp.concatenate([pass_x, w_hi, pass_x, w_hi, zeros_top, w_lo], axis=1)
        w = _bdot(lhs, rhs)
        yield w[:, :, :c], w


class _ChunkOperands(NamedTuple):
    k16: object
    kb16: object
    q16: object
    rhs16: object
    qeg16: object
    kt16: object
    dec: object
    sdec: object
    gate: object


def _operand_scratch(ns):
    nh = ns * HEADS
    return _ChunkOperands(
        k16=pltpu.VMEM((nh, CHUNK, HEAD_DIM), BF16),
        kb16=pltpu.VMEM((nh, CHUNK, HEAD_DIM), BF16),
        q16=pltpu.VMEM((nh, CHUNK, HEAD_DIM), BF16),
        rhs16=pltpu.VMEM((nh, CHUNK, 2 * HEAD_DIM), BF16),
        qeg16=pltpu.VMEM((nh, CHUNK, HEAD_DIM), BF16),
        kt16=pltpu.VMEM((nh, CHUNK, HEAD_DIM), BF16),
        dec=pltpu.VMEM((nh, CHUNK, CHUNK), F32),
        sdec=pltpu.VMEM((nh, 1, HEAD_DIM), F32),
        gate=pltpu.VMEM((ns, CHUNK, D_GDN), F32),
    )


def _prep_tasks(z_ref, zt_ref, wa_ref, wq_ref, alog_ref, dtb_ref, ya_ref, ca_ref, cq_ref,
                buf_a, buf_q, out):
    C = CHUNK
    ns = z_ref.shape[0]
    row = lax.broadcasted_iota(jnp.int32, (C, C), 0)
    col = lax.broadcasted_iota(jnp.int32, (C, C), 1)
    causal = row >= col
    tri = causal.astype(F32)
    sub = lax.broadcasted_iota(jnp.int32, (C // 8, 8, HEAD_DIM), 1)
    never = pl.program_id(1) < 0

    def pinned(row_vec, token):
        return row_vec if token is None else jnp.where(never, token, row_vec)

    def shifted(x3, d):
        r = pltpu.roll(x3, d, axis=1)
        return jnp.where(sub < d, r[:-1], r[1:]).reshape(C, x3.shape[-1])

    def haloed(buf, s, sl):
        x = buf[s, :, sl]
        return x, x.reshape((HALO + C) // 8, 8, HEAD_DIM)

    gates = {}

    def gate_task(s):
        def run(token):
            del token
            beta_s, g_s = _gate_terms(zt_ref[s], alog_ref[...], dtb_ref[...])
            gc_s = _dot_hi(tri, g_s)
            gates[s] = (beta_s, gc_s, gc_s.T)
            buf_a[s, HALO:HALO + C, :] = (z_ref[s, :, OFF_CA:OFF_CA + D_CONV]
                                          * z_ref[s, :, OFF_HA:OFF_HA + D_CONV])
            ca_ref[s] = buf_a[s, HALO + C - 2:HALO + C, :]
            buf_q[s, HALO:HALO + C, :] = z_ref[s, :, OFF_QKV:OFF_QKV + 3 * D_GDN]
            cq_ref[s] = buf_q[s, HALO + C - 3:HALO + C, :]
        return run

    def conv_a_task(s, off):
        def run(token):
            sl = slice(off, off + HEAD_DIM)
            xa, xa3 = haloed(buf_a, s, sl)
            conv = (shifted(xa3, 2) * pinned(wa_ref[0:1, sl], token)
                    + shifted(xa3, 1) * pinned(wa_ref[1:2, sl], token)
                    + xa[HALO:] * pinned(wa_ref[2:3, sl], token))
            ya = (z_ref[s, :, OFF_BA + off:OFF_BA + off + HEAD_DIM] * conv
                  * _silu(z_ref[s, :, OFF_GA + off:OFF_GA + off + HEAD_DIM]))
            ya_ref[s, :, sl] = ya.astype(ya_ref.dtype)
            buf_a[s, 0:HALO, sl] = xa[C:]
        return run

    def conv_qkv(s, off, token):
        sl = slice(off, off + HEAD_DIM)
        x, x3 = haloed(buf_q, s, sl)
        acc = x[HALO:] * pinned(wq_ref[3:4, sl], token)
        for d in range(1, 4):
            acc = acc + shifted(x3, d) * pinned(wq_ref[3 - d:4 - d, sl], token)
        buf_q[s, 0:HALO, sl] = x[C:]
        return _silu(acc)

    def head_task(s, h):
        def run(token):
            n = s * HEADS + h
            beta_s, gc_s, gc_t = gates[s]
            sl = slice(h * HEAD_DIM, (h + 1) * HEAD_DIM)
            q = conv_qkv(s, h * HEAD_DIM, token)
            k = conv_qkv(s, D_GDN + h * HEAD_DIM, token)
            v = conv_qkv(s, 2 * D_GDN + h * HEAD_DIM, token)
            q = q * (lax.rsqrt(jnp.sum(q * q, axis=-1, keepdims=True) + EPS) * (HEAD_DIM ** -0.5))
            k = k * lax.rsqrt(jnp.sum(k * k, axis=-1, keepdims=True) + EPS)
            beta = jnp.broadcast_to(beta_s[:, h:h + 1], (C, HEAD_DIM))
            gcol = jnp.broadcast_to(gc_s[:, HEADS + h:HEADS + h + 1], (C, HEAD_DIM))
            grow = gc_t[HEADS + h:HEADS + h + 1, :]
            glast = gcol[C - 1:C, :]
            eg = jnp.exp(gcol)
            kb = k * beta
            out.k16[n] = k.astype(BF16)
            out.kb16[n] = kb.astype(BF16)
            out.q16[n] = q.astype(BF16)
            out.rhs16[n] = jnp.concatenate([v * beta, kb * eg], axis=1).astype(BF16)
            out.qeg16[n] = (q * eg).astype(BF16)
            out.kt16[n] = (k * jnp.exp(glast - gcol)).astype(BF16)
            out.dec[n] = jnp.exp(jnp.where(causal, gcol[:, :C] - grow, -jnp.inf))
            out.sdec[n] = jnp.exp(glast)
            out.gate[s, :, sl] = _silu(z_ref[s, :, OFF_GB + h * HEAD_DIM:OFF_GB + (h + 1) * HEAD_DIM])
        return run

    tasks = [gate_task(s) for s in range(ns)]
    for s in range(ns):
        for h in range(HEADS):
            tasks.append(head_task(s, h))
            tasks.append(conv_a_task(s, h * HEAD_DIM))
    return tasks


def _scan_stages(ops, on_ref, yb_ref, s_scr):
    C = CHUNK
    ns = yb_ref.shape[0]
    row = lax.broadcasted_iota(jnp.int32, (C, C), 0)
    col = lax.broadcasted_iota(jnp.int32, (C, C), 1)
    strict = (row > col)[None]
    eye = (row == col).astype(F32)[None]

    k16 = ops.k16[...]
    dec_causal = ops.dec[...]
    lmat = _bdot_nt(ops.kb16[...], k16) * jnp.where(strict, dec_causal, 0.0)
    attn = (_bdot_nt(ops.q16[...], k16) * dec_causal).astype(BF16)
    nh = k16.shape[0]
    token_of = lambda a: a[nh - 1, C - 1:C, :HEAD_DIM]
    yield None
    for tmat, w_level in _unit_lower_inverse_levels(lmat, eye):
        yield token_of(w_level)
    uw = _bdot(tmat.astype(BF16), ops.rhs16[...])
    yield token_of(uw)
    s_old = s_scr[...]
    lhs = jnp.concatenate([uw[:, :, HEAD_DIM:].astype(BF16), ops.qeg16[...]], axis=1)
    ws = _bdot(lhs, s_old.astype(BF16))
    yield token_of(ws)
    u = uw[:, :, :HEAD_DIM] - ws[:, :C]
    u16 = u.astype(BF16)
    o = ws[:, C:] + _bdot(attn, u16)
    s_scr[...] = s_old * ops.sdec[...] + _bdot_tn(ops.kt16[...], u16)
    yield token_of(o)
    o = o * lax.rsqrt(jnp.mean(o * o, axis=-1, keepdims=True) + EPS) * on_ref[...]
    for s in range(ns):
        for h in range(HEADS):
            sl = slice(h * HEAD_DIM, (h + 1) * HEAD_DIM)
            yb_ref[s, :, sl] = (o[s * HEADS + h] * ops.gate[s, :, sl]).astype(yb_ref.dtype)


def _interleave(stages, tasks, per_gap):
    todo = list(tasks)
    token = None
    for n, token in zip(list(per_gap) + [0] * 64, stages):
        for task in todo[:n]:
            task(token)
        todo = todo[n:]
    for task in todo:
        task(token)


def _mixer_prompt_kernel(z_ref, zt_ref, wa_ref, wq_ref, alog_ref, dtb_ref, on_ref,
                         ya_ref, yb_ref, ca_ref, cq_ref, ss_ref, buf_a, buf_q, s_scr, *operand_refs):
    t = pl.program_id(1)
    ns = z_ref.shape[0]
    n_fields = len(_ChunkOperands._fields)
    sets = (_ChunkOperands(*operand_refs[:n_fields]), _ChunkOperands(*operand_refs[n_fields:]))

    @pl.when(t == 0)
    def _():
        buf_a[:, 0:HALO, :] = jnp.zeros((ns, HALO, D_CONV), F32)
        buf_q[:, 0:HALO, :] = jnp.zeros((ns, HALO, 3 * D_GDN), F32)
        s_scr[...] = jnp.zeros(s_scr.shape, F32)
        for ref in sets[1]:
            ref[...] = jnp.zeros(ref.shape, ref.dtype)

    for parity in range(2):
        @pl.when(t % 2 == parity)
        def _(parity=parity):
            _interleave(
                _scan_stages(sets[1 - parity], on_ref, yb_ref, s_scr),
                _prep_tasks(z_ref, zt_ref, wa_ref, wq_ref, alog_ref, dtb_ref, ya_ref, ca_ref,
                            cq_ref, buf_a, buf_q, sets[parity]),
                PREP_TASKS_PER_GAP)

    @pl.when(t == pl.num_programs(1) - 1)
    def _():
        ss_ref[...] = s_scr[...].reshape(ss_ref.shape)


def _mixer_prompt_call(layer, z, zt, conv_a_w, conv_qkv_w, alog_rows, dtb_rows, onorm3, batch, seq):
    nc = seq // CHUNK
    ns = PROMPT_STREAMS
    lsel = lambda g, t: (layer, 0, 0)
    cur = lambda g, t: (g, jnp.minimum(t, nc - 1), 0)
    prev = lambda g, t: (g, jnp.maximum(t - 1, 0), 0)
    scratch = _operand_scratch(ns)
    return pl.pallas_call(
        _mixer_prompt_kernel,
        grid=(batch // ns, nc + 1),
        in_specs=[
            pl.BlockSpec((ns, CHUNK, D_MAIN), cur),
            pl.BlockSpec((ns, CHUNK, LANES), cur),
            pl.BlockSpec((None, 3, D_CONV), lsel),
            pl.BlockSpec((None, 4, 3 * D_GDN), lsel),
            pl.BlockSpec((None, 1, LANES), lsel),
            pl.BlockSpec((None, 1, LANES), lsel),
            pl.BlockSpec((None, 1, HEAD_DIM), lsel),
        ],
        out_specs=[
            pl.BlockSpec((ns, CHUNK, D_CONV), lambda g, t: (g, t, 0)),
            pl.BlockSpec((ns, CHUNK, D_GDN), prev),
            pl.BlockSpec((ns, 2, D_CONV), lambda g, t: (g, 0, 0)),
            pl.BlockSpec((ns, 3, 3 * D_GDN), lambda g, t: (g, 0, 0)),
            pl.BlockSpec((ns, HEADS, HEAD_DIM, HEAD_DIM), lambda g, t: (g, 0, 0, 0)),
        ],
        out_shape=[
            jax.ShapeDtypeStruct((batch, seq + CHUNK, D_CONV), BF16),
            jax.ShapeDtypeStruct((batch, seq, D_GDN), BF16),
            jax.ShapeDtypeStruct((batch, 2, D_CONV), F32),
            jax.ShapeDtypeStruct((batch, 3, 3 * D_GDN), F32),
            jax.ShapeDtypeStruct((batch, HEADS, HEAD_DIM, HEAD_DIM), F32),
        ],
        scratch_shapes=[
            pltpu.VMEM((ns, HALO + CHUNK, D_CONV), F32),
            pltpu.VMEM((ns, HALO + CHUNK, 3 * D_GDN), F32),
            pltpu.VMEM((ns * HEADS, HEAD_DIM, HEAD_DIM), F32),
            *scratch, *scratch,
        ],
        compiler_params=pltpu.CompilerParams(
            dimension_semantics=("arbitrary", "arbitrary"), vmem_limit_bytes=VMEM_LIMIT),
        name="mixer_prompt",
    )(z.reshape(batch, seq, D_MAIN), zt.reshape(batch, seq, LANES), conv_a_w, conv_qkv_w,
      alog_rows, dtb_rows, onorm3)


def _mixer_sample_kernel(z_ref, zt_ref, sa_ref, sq_ref, s_ref, wa_ref, wq_ref, alog_ref, dtb_ref,
                         on_ref, y_ref, na_ref, nq_ref, ns_ref):
    nb = z_ref.shape[0]

    ch = z_ref[:, OFF_CA:OFF_CA + D_CONV] * z_ref[:, OFF_HA:OFF_HA + D_CONV]
    prev0 = sa_ref[:, 0:D_CONV]
    prev1 = sa_ref[:, D_CONV:2 * D_CONV]
    conv = prev0 * wa_ref[0:1, :] + prev1 * wa_ref[1:2, :] + ch * wa_ref[2:3, :]
    y_ref[:, 0:D_CONV] = (z_ref[:, OFF_BA:OFF_BA + D_CONV] * conv
                          * _silu(z_ref[:, OFF_GA:OFF_GA + D_CONV]))
    na_ref[:, 0:D_CONV] = prev1
    na_ref[:, D_CONV:2 * D_CONV] = ch

    w3 = 3 * D_GDN
    nq_ref[:, 0:w3] = sq_ref[:, w3:2 * w3]
    nq_ref[:, w3:2 * w3] = sq_ref[:, 2 * w3:3 * w3]
    nq_ref[:, 2 * w3:3 * w3] = z_ref[:, OFF_QKV:OFF_QKV + w3]

    beta_all, g_all = _gate_terms(zt_ref[...], alog_ref[...], dtb_ref[...])
    eg_all = jnp.exp(g_all)
    rowid = lax.broadcasted_iota(jnp.int32, (nb, HEAD_DIM), 0)

    def conv_qkv(off):
        sl = slice(off, off + HEAD_DIM)
        acc = z_ref[:, OFF_QKV + off:OFF_QKV + off + HEAD_DIM] * wq_ref[3:4, sl]
        for j in range(3):
            acc = acc + sq_ref[:, j * w3 + off:j * w3 + off + HEAD_DIM] * wq_ref[j:j + 1, sl]
        return _silu(acc)

    for h in range(HEADS):
        q = conv_qkv(h * HEAD_DIM)
        k = conv_qkv(D_GDN + h * HEAD_DIM)
        v = conv_qkv(2 * D_GDN + h * HEAD_DIM)
        q = q * (lax.rsqrt(jnp.sum(q * q, axis=-1, keepdims=True) + EPS) * (HEAD_DIM ** -0.5))
        k = k * lax.rsqrt(jnp.sum(k * k, axis=-1, keepdims=True) + EPS)
        beta = beta_all[:, h:h + 1]
        eg = eg_all[:, HEADS + h:HEADS + h + 1]
        kq = jnp.concatenate([k, q], axis=0)
        ks = jnp.zeros((nb, HEAD_DIM), F32)
        qs = jnp.zeros((nb, HEAD_DIM), F32)
        for j in range(nb):
            r = _dot(kq, s_ref[j, h])
            ks = jnp.where(rowid == j, r[:nb], ks)
            qs = jnp.where(rowid == j, r[nb:], qs)
        u = beta * (v - eg * ks)
        o = eg * qs + jnp.sum(q * k, axis=-1, keepdims=True) * u
        for j in range(nb):
            outer = _dot_tn(jnp.where(rowid == j, k, 0.0), u)
            ns_ref[j, h] = s_ref[j, h] * eg[j:j + 1, :] + outer
        o = o * lax.rsqrt(jnp.mean(o * o, axis=-1, keepdims=True) + EPS) * on_ref[...]
        gb = z_ref[:, OFF_GB + h * HEAD_DIM:OFF_GB + (h + 1) * HEAD_DIM]
        y_ref[:, D_CONV + h * HEAD_DIM:D_CONV + (h + 1) * HEAD_DIM] = o * _silu(gb)


def _mixer_sample_chained_kernel(prev_ref, *refs):
    del prev_ref
    _mixer_sample_kernel(*refs)


def _mixer_sample_call(layer, z, zt, sa, sq, state_ssm, conv_a_w, conv_qkv_w, alog_rows, dtb_rows,
                       onorm3, ssm_stack):
    nseq = z.shape[0]
    nb = 8
    lsel = lambda i: (layer, 0, 0)
    chained = ssm_stack is not None
    return pl.pallas_call(
        _mixer_sample_chained_kernel if chained else _mixer_sample_kernel,
        grid=(nseq // nb,),
        input_output_aliases={0: 3} if chained else {},
        in_specs=([pl.BlockSpec(memory_space=pl.ANY)] if chained else []) + [
            pl.BlockSpec((nb, D_MAIN), lambda i: (i, 0)),
            pl.BlockSpec((nb, LANES), lambda i: (i, 0)),
            pl.BlockSpec((nb, 2 * D_CONV), lambda i: (i, 0)),
            pl.BlockSpec((nb, 9 * D_GDN), lambda i: (i, 0)),
            pl.BlockSpec((None, nb, HEADS, HEAD_DIM, HEAD_DIM), lambda i: (layer, i, 0, 0, 0)),
            pl.BlockSpec((None, 3, D_CONV), lsel),
            pl.BlockSpec((None, 4, 3 * D_GDN), lsel),
            pl.BlockSpec((None, 1, LANES), lsel),
            pl.BlockSpec((None, 1, LANES), lsel),
            pl.BlockSpec((None, 1, HEAD_DIM), lsel),
        ],
        out_specs=[
            pl.BlockSpec((nb, D_MODEL), lambda i: (i, 0)),
            pl.BlockSpec((nb, 2 * D_CONV), lambda i: (i, 0)),
            pl.BlockSpec((nb, 9 * D_GDN), lambda i: (i, 0)),
            pl.BlockSpec((None, nb, HEADS, HEAD_DIM, HEAD_DIM), lambda i: (layer, i, 0, 0, 0)),
        ],
        out_shape=[
            jax.ShapeDtypeStruct((nseq, D_MODEL), F32),
            jax.ShapeDtypeStruct((nseq, 2 * D_CONV), F32),
            jax.ShapeDtypeStruct((nseq, 9 * D_GDN), F32),
            jax.ShapeDtypeStruct((DEPTH, nseq, HEADS, HEAD_DIM, HEAD_DIM), F32),
        ],
        compiler_params=pltpu.CompilerParams(
            dimension_semantics=("arbitrary",), vmem_limit_bytes=VMEM_LIMIT),
        name="mixer_sample",
    )(*([ssm_stack] if chained else []), z, zt, sa, sq, state_ssm, conv_a_w, conv_qkv_w,
      alog_rows, dtb_rows, onorm3)


def kernel(x_prompt, x_sample, state_conv_a, state_conv_qkv, state_ssm, c_prompt, c_sample, norm_g, w_ada, b_ada, w_in, conv_a_w, conv_qkv_w, a_log, dt_bias, o_norm_g, w_out, final_norm_g):
    batch, seq, _ = x_prompt.shape
    nseq = x_sample.shape[0]
    assert x_sample.shape[1] == 1 and seq % CHUNK == 0

    w_main = w_in
    w_tail = jnp.pad(w_in[:, :, D_MAIN:], ((0, 0), (0, 0), (0, LANES - 2 * HEADS))).astype(BF16)
    w_out16 = w_out.astype(BF16)
    norm_g3 = norm_g.reshape(DEPTH, 1, D_MODEL)
    onorm3 = o_norm_g.reshape(DEPTH, 1, HEAD_DIM)
    final_g = final_norm_g.reshape(1, D_MODEL)
    pad_heads = ((0, 0), (HEADS, LANES - 2 * HEADS))
    alog_rows = jnp.pad(a_log, pad_heads).reshape(DEPTH, 1, LANES)
    dtb_rows = jnp.pad(dt_bias, pad_heads).reshape(DEPTH, 1, LANES)

    n_cond = nseq + batch
    n_cond_pad = -(-n_cond // 8) * 8
    c_all = jnp.concatenate(
        [c_sample, c_prompt, jnp.zeros((n_cond_pad - n_cond, D_MODEL), F32)], axis=0)
    mod = _mod_call(c_all, w_ada, b_ada)
    mod4 = mod.reshape(DEPTH, n_cond_pad, 1, 3 * D_MODEL)

    xp = x_prompt.reshape(batch * seq, D_MODEL)
    xs = x_sample.reshape(nseq, D_MODEL)
    tm_in = 1024
    tm_out = 512
    outs = {k: [] for k in ("ca_p", "cq_p", "ss_p", "ca_s", "cq_s")}
    ssm_stack = None

    for layer in range(DEPTH):
        def pmod(sec, per_batch, layer=layer):
            return pl.BlockSpec((None, None, 1, D_MODEL),
                                lambda i, *_: (layer, nseq + i // per_batch, 0, sec))
        z, zt = _inproj_call(layer, xp, norm_g3, mod4,
                             (pmod(1, seq // tm_in), pmod(0, seq // tm_in)), w_main, w_tail, tm_in)
        ya, yb, ca, cq, ss = _mixer_prompt_call(layer, z, zt, conv_a_w, conv_qkv_w, alog_rows,
                                                dtb_rows, onorm3, batch, seq)
        per_seq = seq // tm_out
        y_specs = [pl.BlockSpec((None, tm_out, D_CONV), lambda i: (i // per_seq, i % per_seq, 0)),
                   pl.BlockSpec((None, tm_out, D_GDN), lambda i: (i // per_seq, i % per_seq, 0))]
        xp = _outproj_call(layer, ya, yb, y_specs, xp, mod4, pmod(2, per_seq), w_out16, final_g,
                           tm_out)
        outs["ca_p"].append(ca)
        outs["cq_p"].append(cq)
        outs["ss_p"].append(ss)

        def smod(sec, layer=layer):
            return pl.BlockSpec((None, nseq, D_MODEL), lambda i, *_: (layer, 0, sec))
        z, zt = _inproj_call(layer, xs, norm_g3, mod, (smod(1), smod(0)), w_main, w_tail, nseq)
        sa = state_conv_a[layer].reshape(nseq, 2 * D_CONV)
        sq = state_conv_qkv[layer].reshape(nseq, 9 * D_GDN)
        y, na, nq, ssm_stack = _mixer_sample_call(layer, z, zt, sa, sq, state_ssm, conv_a_w,
                                                  conv_qkv_w, alog_rows, dtb_rows, onorm3, ssm_stack)
        y_specs = [pl.BlockSpec((nseq, D_CONV), lambda i: (0, 0)),
                   pl.BlockSpec((nseq, D_GDN), lambda i: (0, 1))]
        xs = _outproj_call(layer, y, y, y_specs, xs, mod, smod(2), w_out16, final_g, nseq)
        outs["ca_s"].append(na.reshape(nseq, 2, D_CONV))
        outs["cq_s"].append(nq.reshape(nseq, 3, 3 * D_GDN))

    return (xp.reshape(batch, seq, D_MODEL), xs.reshape(nseq, 1, D_MODEL),
            jnp.stack(outs["ca_p"]), jnp.stack(outs["cq_p"]), jnp.stack(outs["ss_p"]),
            jnp.stack(outs["ca_s"]), jnp.stack(outs["cq_s"]), ssm_stack)
```

```python
import functools
from typing import NamedTuple

import jax
import jax.numpy as jnp
from jax import lax
from jax.experimental import pallas as pl
from jax.experimental.pallas import tpu as pltpu

F32 = jnp.float32
BF16 = jnp.bfloat16

D_MODEL = 2048
DEPTH = 4
D_CONV = 1024
D_GDN = 1024
HEADS = 8
HEAD_DIM = 128
CHUNK = 64
EPS = 1e-6
D_MAIN = 8192
OFF_BA, OFF_CA, OFF_HA, OFF_GA, OFF_QKV, OFF_GB = 0, 1024, 2048, 3072, 4096, 7168
LANES = 128
HALO = 8
VMEM_LIMIT = 56 * 1024 * 1024
VMEM_LIMIT_IN_PROJ = 60 * 1024 * 1024
PROMPT_STREAMS = 2
PREP_TASKS_PER_GAP = (8, 5, 5, 5, 4, 3, 2, 2, 0, 0)


def _silu_of_half(hx):
    return hx + hx * jnp.tanh(hx)


def _silu(x):
    return _silu_of_half(0.5 * x)


def _softplus(x):
    return jnp.maximum(x, 0.0) + jnp.log1p(jnp.exp(-jnp.abs(x)))


def _dot(a, b):
    return jnp.dot(a, b, preferred_element_type=F32)


def _dot_tn(a, b):
    return lax.dot_general(a, b, (((0,), (0,)), ((), ())), preferred_element_type=F32)


def _dot_hi(a, b):
    return jnp.dot(a, b, preferred_element_type=F32, precision=lax.Precision.HIGHEST)


def _mod_kernel(c_ref, w_ref, b_ref, o_ref):
    s = _silu(c_ref[...]).astype(BF16)
    o_ref[...] = _dot(s, w_ref[...].astype(BF16)) + b_ref[...]


def _mod_call(c_all, w_ada, b_ada):
    rows = c_all.shape[0]
    tn = 1024
    return pl.pallas_call(
        _mod_kernel,
        grid=(DEPTH, 3 * D_MODEL // tn),
        in_specs=[
            pl.BlockSpec((rows, D_MODEL), lambda l, j: (0, 0)),
            pl.BlockSpec((None, D_MODEL, tn), lambda l, j: (l, 0, j)),
            pl.BlockSpec((None, 1, tn), lambda l, j: (l, 0, j)),
        ],
        out_specs=pl.BlockSpec((None, rows, tn), lambda l, j: (l, 0, j)),
        out_shape=jax.ShapeDtypeStruct((DEPTH, rows, 3 * D_MODEL), F32),
        compiler_params=pltpu.CompilerParams(
            dimension_semantics=("arbitrary", "arbitrary"), vmem_limit_bytes=VMEM_LIMIT),
        name="adaln_mod",
    )(c_all, w_ada, b_ada.reshape(DEPTH, 1, 3 * D_MODEL))


def _inproj_kernel(x_ref, g_ref, sc_ref, sh_ref, w_ref, wt_ref, z_ref, zt_ref, h_scr):
    @pl.when(pl.program_id(1) == 0)
    def _():
        x = x_ref[...]
        y = x * lax.rsqrt(jnp.mean(x * x, axis=-1, keepdims=True) + EPS) * g_ref[...]
        h = (y * (1.0 + sc_ref[...]) + sh_ref[...]).astype(BF16)
        h_scr[...] = h
        zt_ref[...] = _dot(h, wt_ref[...])

    z = _dot(h_scr[...], w_ref[...])
    gate_tile = pl.program_id(1) % 2 == 1
    z_ref[:, :D_CONV] = z[:, :D_CONV]
    z_ref[:, D_CONV:] = jnp.where(gate_tile, _silu(z[:, D_CONV:]), z[:, D_CONV:])


def _inproj_call(layer, x, norm_g3, mod_arr, mod_specs, w_main, w_tail, tm):
    rows = x.shape[0]
    tn = 2 * D_CONV
    sc_spec, sh_spec = mod_specs
    return pl.pallas_call(
        _inproj_kernel,
        grid=(rows // tm, D_MAIN // tn),
        in_specs=[
            pl.BlockSpec((tm, D_MODEL), lambda i, j: (i, 0)),
            pl.BlockSpec((None, 1, D_MODEL), lambda i, j: (layer, 0, 0)),
            sc_spec,
            sh_spec,
            pl.BlockSpec((None, D_MODEL, tn), lambda i, j: (layer, 0, j)),
            pl.BlockSpec((None, D_MODEL, LANES), lambda i, j: (layer, 0, 0)),
        ],
        out_specs=[
            pl.BlockSpec((tm, tn), lambda i, j: (i, j)),
            pl.BlockSpec((tm, LANES), lambda i, j: (i, 0)),
        ],
        out_shape=[
            jax.ShapeDtypeStruct((rows, D_MAIN), F32),
            jax.ShapeDtypeStruct((rows, LANES), F32),
        ],
        scratch_shapes=[pltpu.VMEM((tm, D_MODEL), BF16)],
        compiler_params=pltpu.CompilerParams(
            dimension_semantics=("arbitrary", "arbitrary"), vmem_limit_bytes=VMEM_LIMIT_IN_PROJ),
        name="in_proj",
    )(x, norm_g3, mod_arr, mod_arr, w_main, w_tail)


def _outproj_kernel(ya_ref, yb_ref, x_ref, gate_ref, wa_ref, wb_ref, fg_ref, o_ref, *, final):
    acc = (_dot(ya_ref[...].astype(BF16), wa_ref[...])
           + _dot(yb_ref[...].astype(BF16), wb_ref[...]))
    xn = x_ref[...] + gate_ref[...] * acc
    if final:
        xn = xn * lax.rsqrt(jnp.mean(xn * xn, axis=-1, keepdims=True) + EPS) * fg_ref[...]
    o_ref[...] = xn


def _outproj_call(layer, ya, yb, y_specs, x, mod_arr, gate_spec, w_out, final_g, tm):
    rows = x.shape[0]
    half = D_MODEL // 2
    return pl.pallas_call(
        functools.partial(_outproj_kernel, final=(layer == DEPTH - 1)),
        grid=(rows // tm,),
        in_specs=[
            *y_specs,
            pl.BlockSpec((tm, D_MODEL), lambda i: (i, 0)),
            gate_spec,
            pl.BlockSpec((None, half, D_MODEL), lambda i: (layer, 0, 0)),
            pl.BlockSpec((None, half, D_MODEL), lambda i: (layer, 1, 0)),
            pl.BlockSpec((1, D_MODEL), lambda i: (0, 0)),
        ],
        out_specs=pl.BlockSpec((tm, D_MODEL), lambda i: (i, 0)),
        out_shape=jax.ShapeDtypeStruct((rows, D_MODEL), F32),
        compiler_params=pltpu.CompilerParams(
            dimension_semantics=("arbitrary",), vmem_limit_bytes=VMEM_LIMIT),
        name="out_proj",
    )(ya, yb, x, mod_arr, w_out, w_out, final_g)


def _gate_terms(tail, alog_row, dtb_row):
    beta = jax.nn.sigmoid(tail)
    g = -jnp.exp(alog_row) * _softplus(tail + dtb_row)
    return beta, g


def _bdot(a, b):
    return lax.dot_general(a, b, (((2,), (1,)), ((0,), (0,))), preferred_element_type=F32)


def _bdot_nt(a, b):
    return lax.dot_general(a, b, (((2,), (2,)), ((0,), (0,))), preferred_element_type=F32)


def _bdot_tn(a, b):
    return lax.dot_general(a, b, (((1,), (1,)), ((0,), (0,))), preferred_element_type=F32)


def _split_bf16(a):
    hi = a.astype(BF16)
    lo = (a - hi.astype(F32)).astype(BF16)
    return hi, lo


def _unit_lower_inverse_levels(lmat, eye):
    heads, c, _ = lmat.shape
    eye_b = jnp.broadcast_to(eye, lmat.shape)
    pass_x = jnp.concatenate([eye_b, jnp.zeros_like(eye_b)], axis=2).astype(BF16)
    zeros_top = jnp.zeros((heads, c, 2 * c), BF16)
    w = jnp.concatenate([eye_b, -lmat], axis=2)
    for _ in range(6):
        w_hi, w_lo = _split_bf16(w)
        lhs = jnp.concatenate([w_hi, w_lo, w_hi], axis=2)
        rhs = jnp.concatenate([pass_x, w_hi, pass_x, w_hi, zeros_top, w_lo], axis=1)
        w = _bdot(lhs, rhs)
        yield w[:, :, :c], w


class _ChunkOperands(NamedTuple):
    k16: object
    kb16: object
    q16: object
    rhs16: object
    qeg16: object
    kt16: object
    dec: object
    sdec: object
    gate: object


def _operand_scratch(ns):
    nh = ns * HEADS
    return _ChunkOperands(
        k16=pltpu.VMEM((nh, CHUNK, HEAD_DIM), BF16),
        kb16=pltpu.VMEM((nh, CHUNK, HEAD_DIM), BF16),
        q16=pltpu.VMEM((nh, CHUNK, HEAD_DIM), BF16),
        rhs16=pltpu.VMEM((nh, CHUNK, 2 * HEAD_DIM), BF16),
        qeg16=pltpu.VMEM((nh, CHUNK, HEAD_DIM), BF16),
        kt16=pltpu.VMEM((nh, CHUNK, HEAD_DIM), BF16),
        dec=pltpu.VMEM((nh, CHUNK, CHUNK), F32),
        sdec=pltpu.VMEM((nh, 1, HEAD_DIM), F32),
        gate=pltpu.VMEM((ns, CHUNK, D_GDN), F32),
    )


def _prep_tasks(z_ref, zt_ref, wa_ref, wq_ref, alog_ref, dtb_ref, ya_ref, ca_ref, cq_ref,
                buf_a, buf_q, out):
    C = CHUNK
    ns = z_ref.shape[0]
    row = lax.broadcasted_iota(jnp.int32, (C, C), 0)
    col = lax.broadcasted_iota(jnp.int32, (C, C), 1)
    causal = row >= col
    tri = causal.astype(F32)
    sub = lax.broadcasted_iota(jnp.int32, (C // 8, 8, HEAD_DIM), 1)
    never = pl.program_id(1) < 0

    def pinned(row_vec, token):
        return row_vec if token is None else jnp.where(never, token, row_vec)

    def shifted(x3, d):
        r = pltpu.roll(x3, d, axis=1)
        return jnp.where(sub < d, r[:-1], r[1:]).reshape(C, x3.shape[-1])

    def haloed(buf, s, sl):
        x = buf[s, :, sl]
        return x, x.reshape((HALO + C) // 8, 8, HEAD_DIM)

    gates = {}

    def gate_task(s):
        def run(token):
            del token
            beta_s, g_s = _gate_terms(zt_ref[s], alog_ref[...], dtb_ref[...])
            gc_s = _dot_hi(tri, g_s)
            gates[s] = (beta_s, gc_s, gc_s.T)
            buf_a[s, HALO:HALO + C, :] = (z_ref[s, :, OFF_CA:OFF_CA + D_CONV]
                                          * z_ref[s, :, OFF_HA:OFF_HA + D_CONV])
            ca_ref[s] = buf_a[s, HALO + C - 2:HALO + C, :]
            buf_q[s, HALO:HALO + C, :] = z_ref[s, :, OFF_QKV:OFF_QKV + 3 * D_GDN]
            cq_ref[s] = buf_q[s, HALO + C - 3:HALO + C, :]
        return run

    def conv_a_task(s, off):
        def run(token):
            sl = slice(off, off + HEAD_DIM)
            xa, xa3 = haloed(buf_a, s, sl)
            conv = (shifted(xa3, 2) * pinned(wa_ref[0:1, sl], token)
                    + shifted(xa3, 1) * pinned(wa_ref[1:2, sl], token)
                    + xa[HALO:] * pinned(wa_ref[2:3, sl], token))
            ya = (z_ref[s, :, OFF_BA + off:OFF_BA + off + HEAD_DIM] * conv
                  * z_ref[s, :, OFF_GA + off:OFF_GA + off + HEAD_DIM])
            ya_ref[s, :, sl] = ya.astype(ya_ref.dtype)
            buf_a[s, 0:HALO, sl] = xa[C:]
        return run

    def conv_qkv(s, off, token):
        sl = slice(off, off + HEAD_DIM)
        x, x3 = haloed(buf_q, s, sl)
        acc = x[HALO:] * pinned(wq_ref[3:4, sl], token)
        for d in range(1, 4):
            acc = acc + shifted(x3, d) * pinned(wq_ref[3 - d:4 - d, sl], token)
        buf_q[s, 0:HALO, sl] = x[C:]
        return _silu_of_half(acc)

    def head_task(s, h):
        def run(token):
            n = s * HEADS + h
            beta_s, gc_s, gc_t = gates[s]
            sl = slice(h * HEAD_DIM, (h + 1) * HEAD_DIM)
            q = conv_qkv(s, h * HEAD_DIM, token)
            k = conv_qkv(s, D_GDN + h * HEAD_DIM, token)
            v = conv_qkv(s, 2 * D_GDN + h * HEAD_DIM, token)
            q = q * (lax.rsqrt(jnp.sum(q * q, axis=-1, keepdims=True) + EPS) * (HEAD_DIM ** -0.5))
            k = k * lax.rsqrt(jnp.sum(k * k, axis=-1, keepdims=True) + EPS)
            beta = jnp.broadcast_to(beta_s[:, h:h + 1], (C, HEAD_DIM))
            gcol = jnp.broadcast_to(gc_s[:, HEADS + h:HEADS + h + 1], (C, HEAD_DIM))
            grow = gc_t[HEADS + h:HEADS + h + 1, :]
            glast = gcol[C - 1:C, :]
            eg = jnp.exp(gcol)
            kb = k * beta
            out.k16[n] = k.astype(BF16)
            out.kb16[n] = kb.astype(BF16)
            out.q16[n] = q.astype(BF16)
            out.rhs16[n] = jnp.concatenate([v * beta, kb * eg], axis=1).astype(BF16)
            out.qeg16[n] = (q * eg).astype(BF16)
            out.kt16[n] = (k * jnp.exp(glast - gcol)).astype(BF16)
            out.dec[n] = jnp.exp(jnp.where(causal, gcol[:, :C] - grow, -jnp.inf))
            out.sdec[n] = jnp.exp(glast)
            out.gate[s, :, sl] = z_ref[s, :, OFF_GB + h * HEAD_DIM:OFF_GB + (h + 1) * HEAD_DIM]
        return run

    tasks = [gate_task(s) for s in range(ns)]
    for s in range(ns):
        for h in range(HEADS):
            tasks.append(head_task(s, h))
            tasks.append(conv_a_task(s, h * HEAD_DIM))
    return tasks


def _scan_stages(ops, on_ref, yb_ref, s_scr):
    C = CHUNK
    ns = yb_ref.shape[0]
    row = lax.broadcasted_iota(jnp.int32, (C, C), 0)
    col = lax.broadcasted_iota(jnp.int32, (C, C), 1)
    strict = (row > col)[None]
    eye = (row == col).astype(F32)[None]

    k16 = ops.k16[...]
    dec_causal = ops.dec[...]
    lmat = _bdot_nt(ops.kb16[...], k16) * jnp.where(strict, dec_causal, 0.0)
    attn = (_bdot_nt(ops.q16[...], k16) * dec_causal).astype(BF16)
    nh = k16.shape[0]
    token_of = lambda a: a[nh - 1, C - 1:C, :HEAD_DIM]
    yield None
    for tmat, w_level in _unit_lower_inverse_levels(lmat, eye):
        yield token_of(w_level)
    uw = _bdot(tmat.astype(BF16), ops.rhs16[...])
    yield token_of(uw)
    s_old = s_scr[...]
    lhs = jnp.concatenate([uw[:, :, HEAD_DIM:].astype(BF16), ops.qeg16[...]], axis=1)
    ws = _bdot(lhs, s_old.astype(BF16))
    yield token_of(ws)
    u = uw[:, :, :HEAD_DIM] - ws[:, :C]
    u16 = u.astype(BF16)
    o = ws[:, C:] + _bdot(attn, u16)
    s_scr[...] = s_old * ops.sdec[...] + _bdot_tn(ops.kt16[...], u16)
    yield token_of(o)
    o = o * lax.rsqrt(jnp.mean(o * o, axis=-1, keepdims=True) + EPS) * on_ref[...]
    for s in range(ns):
        for h in range(HEADS):
            sl = slice(h * HEAD_DIM, (h + 1) * HEAD_DIM)
            yb_ref[s, :, sl] = (o[s * HEADS + h] * ops.gate[s, :, sl]).astype(yb_ref.dtype)


def _interleave(stages, tasks, per_gap):
    todo = list(tasks)
    token = None
    for n, token in zip(list(per_gap) + [0] * 64, stages):
        for task in todo[:n]:
            task(token)
        todo = todo[n:]
    for task in todo:
        task(token)


def _mixer_prompt_kernel(z_ref, zt_ref, wa_ref, wq_ref, alog_ref, dtb_ref, on_ref,
                         ya_ref, yb_ref, ca_ref, cq_ref, ss_ref, buf_a, buf_q, s_scr, *operand_refs):
    t = pl.program_id(1)
    ns = z_ref.shape[0]
    n_fields = len(_ChunkOperands._fields)
    sets = (_ChunkOperands(*operand_refs[:n_fields]), _ChunkOperands(*operand_refs[n_fields:]))

    @pl.when(t == 0)
    def _():
        buf_a[:, 0:HALO, :] = jnp.zeros((ns, HALO, D_CONV), F32)
        buf_q[:, 0:HALO, :] = jnp.zeros((ns, HALO, 3 * D_GDN), F32)
        s_scr[...] = jnp.zeros(s_scr.shape, F32)
        for ref in sets[1]:
            ref[...] = jnp.zeros(ref.shape, ref.dtype)

    for parity in range(2):
        @pl.when(t % 2 == parity)
        def _(parity=parity):
            _interleave(
                _scan_stages(sets[1 - parity], on_ref, yb_ref, s_scr),
                _prep_tasks(z_ref, zt_ref, wa_ref, wq_ref, alog_ref, dtb_ref, ya_ref, ca_ref,
                            cq_ref, buf_a, buf_q, sets[parity]),
                PREP_TASKS_PER_GAP)

    @pl.when(t == pl.num_programs(1) - 1)
    def _():
        ss_ref[...] = s_scr[...].reshape(ss_ref.shape)


def _mixer_prompt_call(layer, z, zt, conv_a_w, conv_qkv_w, alog_rows, dtb_rows, onorm3, batch, seq):
    nc = seq // CHUNK
    ns = PROMPT_STREAMS
    lsel = lambda g, t: (layer, 0, 0)
    cur = lambda g, t: (g, jnp.minimum(t, nc - 1), 0)
    prev = lambda g, t: (g, jnp.maximum(t - 1, 0), 0)
    scratch = _operand_scratch(ns)
    return pl.pallas_call(
        _mixer_prompt_kernel,
        grid=(batch // ns, nc + 1),
        in_specs=[
            pl.BlockSpec((ns, CHUNK, D_MAIN), cur),
            pl.BlockSpec((ns, CHUNK, LANES), cur),
            pl.BlockSpec((None, 3, D_CONV), lsel),
            pl.BlockSpec((None, 4, 3 * D_GDN), lsel),
            pl.BlockSpec((None, 1, LANES), lsel),
            pl.BlockSpec((None, 1, LANES), lsel),
            pl.BlockSpec((None, 1, HEAD_DIM), lsel),
        ],
        out_specs=[
            pl.BlockSpec((ns, CHUNK, D_CONV), lambda g, t: (g, t, 0)),
            pl.BlockSpec((ns, CHUNK, D_GDN), prev),
            pl.BlockSpec((ns, 2, D_CONV), lambda g, t: (g, 0, 0)),
            pl.BlockSpec((ns, 3, 3 * D_GDN), lambda g, t: (g, 0, 0)),
            pl.BlockSpec((ns, HEADS, HEAD_DIM, HEAD_DIM), lambda g, t: (g, 0, 0, 0)),
        ],
        out_shape=[
            jax.ShapeDtypeStruct((batch, seq + CHUNK, D_CONV), BF16),
            jax.ShapeDtypeStruct((batch, seq, D_GDN), BF16),
            jax.ShapeDtypeStruct((batch, 2, D_CONV), F32),
            jax.ShapeDtypeStruct((batch, 3, 3 * D_GDN), F32),
            jax.ShapeDtypeStruct((batch, HEADS, HEAD_DIM, HEAD_DIM), F32),
        ],
        scratch_shapes=[
            pltpu.VMEM((ns, HALO + CHUNK, D_CONV), F32),
            pltpu.VMEM((ns, HALO + CHUNK, 3 * D_GDN), F32),
            pltpu.VMEM((ns * HEADS, HEAD_DIM, HEAD_DIM), F32),
            *scratch, *scratch,
        ],
        compiler_params=pltpu.CompilerParams(
            dimension_semantics=("arbitrary", "arbitrary"), vmem_limit_bytes=VMEM_LIMIT),
        name="mixer_prompt",
    )(z.reshape(batch, seq, D_MAIN), zt.reshape(batch, seq, LANES), conv_a_w, conv_qkv_w,
      alog_rows, dtb_rows, onorm3)


def _mixer_sample_kernel(z_ref, zt_ref, sa_ref, sq_ref, s_ref, wa_ref, wq_ref, alog_ref, dtb_ref,
                         on_ref, y_ref, na_ref, nq_ref, ns_ref):
    nb = z_ref.shape[0]

    ch = z_ref[:, OFF_CA:OFF_CA + D_CONV] * z_ref[:, OFF_HA:OFF_HA + D_CONV]
    prev0 = sa_ref[:, 0:D_CONV]
    prev1 = sa_ref[:, D_CONV:2 * D_CONV]
    conv = prev0 * wa_ref[0:1, :] + prev1 * wa_ref[1:2, :] + ch * wa_ref[2:3, :]
    y_ref[:, 0:D_CONV] = (z_ref[:, OFF_BA:OFF_BA + D_CONV] * conv
                          * z_ref[:, OFF_GA:OFF_GA + D_CONV])
    na_ref[:, 0:D_CONV] = prev1
    na_ref[:, D_CONV:2 * D_CONV] = ch

    w3 = 3 * D_GDN
    nq_ref[:, 0:w3] = sq_ref[:, w3:2 * w3]
    nq_ref[:, w3:2 * w3] = sq_ref[:, 2 * w3:3 * w3]
    nq_ref[:, 2 * w3:3 * w3] = z_ref[:, OFF_QKV:OFF_QKV + w3]

    beta_all, g_all = _gate_terms(zt_ref[...], alog_ref[...], dtb_ref[...])
    eg_all = jnp.exp(g_all)
    rowid = lax.broadcasted_iota(jnp.int32, (nb, HEAD_DIM), 0)

    def conv_qkv(off):
        sl = slice(off, off + HEAD_DIM)
        acc = z_ref[:, OFF_QKV + off:OFF_QKV + off + HEAD_DIM] * wq_ref[3:4, sl]
        for j in range(3):
            acc = acc + sq_ref[:, j * w3 + off:j * w3 + off + HEAD_DIM] * wq_ref[j:j + 1, sl]
        return _silu_of_half(acc)

    for h in range(HEADS):
        q = conv_qkv(h * HEAD_DIM)
        k = conv_qkv(D_GDN + h * HEAD_DIM)
        v = conv_qkv(2 * D_GDN + h * HEAD_DIM)
        q = q * (lax.rsqrt(jnp.sum(q * q, axis=-1, keepdims=True) + EPS) * (HEAD_DIM ** -0.5))
        k = k * lax.rsqrt(jnp.sum(k * k, axis=-1, keepdims=True) + EPS)
        beta = beta_all[:, h:h + 1]
        eg = eg_all[:, HEADS + h:HEADS + h + 1]
        kq = jnp.concatenate([k, q], axis=0)
        ks = jnp.zeros((nb, HEAD_DIM), F32)
        qs = jnp.zeros((nb, HEAD_DIM), F32)
        for j in range(nb):
            r = _dot(kq, s_ref[j, h])
            ks = jnp.where(rowid == j, r[:nb], ks)
            qs = jnp.where(rowid == j, r[nb:], qs)
        u = beta * (v - eg * ks)
        o = eg * qs + jnp.sum(q * k, axis=-1, keepdims=True) * u
        for j in range(nb):
            outer = _dot_tn(jnp.where(rowid == j, k, 0.0), u)
            ns_ref[j, h] = s_ref[j, h] * eg[j:j + 1, :] + outer
        o = o * lax.rsqrt(jnp.mean(o * o, axis=-1, keepdims=True) + EPS) * on_ref[...]
        gb = z_ref[:, OFF_GB + h * HEAD_DIM:OFF_GB + (h + 1) * HEAD_DIM]
        y_ref[:, D_CONV + h * HEAD_DIM:D_CONV + (h + 1) * HEAD_DIM] = o * gb


def _mixer_sample_chained_kernel(prev_ref, *refs):
    del prev_ref
    _mixer_sample_kernel(*refs)


def _mixer_sample_call(layer, z, zt, sa, sq, state_ssm, conv_a_w, conv_qkv_w, alog_rows, dtb_rows,
                       onorm3, ssm_stack):
    nseq = z.shape[0]
    nb = 8
    lsel = lambda i: (layer, 0, 0)
    chained = ssm_stack is not None
    return pl.pallas_call(
        _mixer_sample_chained_kernel if chained else _mixer_sample_kernel,
        grid=(nseq // nb,),
        input_output_aliases={0: 3} if chained else {},
        in_specs=([pl.BlockSpec(memory_space=pl.ANY)] if chained else []) + [
            pl.BlockSpec((nb, D_MAIN), lambda i: (i, 0)),
            pl.BlockSpec((nb, LANES), lambda i: (i, 0)),
            pl.BlockSpec((nb, 2 * D_CONV), lambda i: (i, 0)),
            pl.BlockSpec((nb, 9 * D_GDN), lambda i: (i, 0)),
            pl.BlockSpec((None, nb, HEADS, HEAD_DIM, HEAD_DIM), lambda i: (layer, i, 0, 0, 0)),
            pl.BlockSpec((None, 3, D_CONV), lsel),
            pl.BlockSpec((None, 4, 3 * D_GDN), lsel),
            pl.BlockSpec((None, 1, LANES), lsel),
            pl.BlockSpec((None, 1, LANES), lsel),
            pl.BlockSpec((None, 1, HEAD_DIM), lsel),
        ],
        out_specs=[
            pl.BlockSpec((nb, D_MODEL), lambda i: (i, 0)),
            pl.BlockSpec((nb, 2 * D_CONV), lambda i: (i, 0)),
            pl.BlockSpec((nb, 9 * D_GDN), lambda i: (i, 0)),
            pl.BlockSpec((None, nb, HEADS, HEAD_DIM, HEAD_DIM), lambda i: (layer, i, 0, 0, 0)),
        ],
        out_shape=[
            jax.ShapeDtypeStruct((nseq, D_MODEL), F32),
            jax.ShapeDtypeStruct((nseq, 2 * D_CONV), F32),
            jax.ShapeDtypeStruct((nseq, 9 * D_GDN), F32),
            jax.ShapeDtypeStruct((DEPTH, nseq, HEADS, HEAD_DIM, HEAD_DIM), F32),
        ],
        compiler_params=pltpu.CompilerParams(
            dimension_semantics=("arbitrary",), vmem_limit_bytes=VMEM_LIMIT),
        name="mixer_sample",
    )(*([ssm_stack] if chained else []), z, zt, sa, sq, state_ssm, conv_a_w, conv_qkv_w,
      alog_rows, dtb_rows, onorm3)


def kernel(x_prompt, x_sample, state_conv_a, state_conv_qkv, state_ssm, c_prompt, c_sample, norm_g, w_ada, b_ada, w_in, conv_a_w, conv_qkv_w, a_log, dt_bias, o_norm_g, w_out, final_norm_g):
    batch, seq, _ = x_prompt.shape
    nseq = x_sample.shape[0]
    assert x_sample.shape[1] == 1 and seq % CHUNK == 0

    w_main = w_in.astype(BF16)
    w_tail = jnp.pad(w_in[:, :, D_MAIN:], ((0, 0), (0, 0), (0, LANES - 2 * HEADS))).astype(BF16)
    w_out16 = w_out.astype(BF16)
    norm_g3 = norm_g.reshape(DEPTH, 1, D_MODEL)
    onorm3 = o_norm_g.reshape(DEPTH, 1, HEAD_DIM)
    final_g = final_norm_g.reshape(1, D_MODEL)
    conv_qkv_half = 0.5 * conv_qkv_w
    pad_heads = ((0, 0), (HEADS, LANES - 2 * HEADS))
    alog_rows = jnp.pad(a_log, pad_heads).reshape(DEPTH, 1, LANES)
    dtb_rows = jnp.pad(dt_bias, pad_heads).reshape(DEPTH, 1, LANES)

    n_cond = nseq + batch
    n_cond_pad = -(-n_cond // 8) * 8
    c_all = jnp.concatenate(
        [c_sample, c_prompt, jnp.zeros((n_cond_pad - n_cond, D_MODEL), F32)], axis=0)
    mod = _mod_call(c_all, w_ada, b_ada)
    mod4 = mod.reshape(DEPTH, n_cond_pad, 1, 3 * D_MODEL)

    xp = x_prompt.reshape(batch * seq, D_MODEL)
    xs = x_sample.reshape(nseq, D_MODEL)
    tm_in = 1024
    tm_out = 512
    outs = {k: [] for k in ("ca_p", "cq_p", "ss_p", "ca_s", "cq_s")}
    ssm_stack = None

    for layer in range(DEPTH):
        def pmod(sec, per_batch, layer=layer):
            return pl.BlockSpec((None, None, 1, D_MODEL),
                                lambda i, *_: (layer, nseq + i // per_batch, 0, sec))
        z, zt = _inproj_call(layer, xp, norm_g3, mod4,
                             (pmod(1, seq // tm_in), pmod(0, seq // tm_in)), w_main, w_tail, tm_in)
        ya, yb, ca, cq, ss = _mixer_prompt_call(layer, z, zt, conv_a_w, conv_qkv_half, alog_rows,
                                                dtb_rows, onorm3, batch, seq)
        per_seq = seq // tm_out
        y_specs = [pl.BlockSpec((None, tm_out, D_CONV), lambda i: (i // per_seq, i % per_seq, 0)),
                   pl.BlockSpec((None, tm_out, D_GDN), lambda i: (i // per_seq, i % per_seq, 0))]
        xp = _outproj_call(layer, ya, yb, y_specs, xp, mod4, pmod(2, per_seq), w_out16, final_g,
                           tm_out)
        outs["ca_p"].append(ca)
        outs["cq_p"].append(cq)
        outs["ss_p"].append(ss)

        def smod(sec, layer=layer):
            return pl.BlockSpec((None, nseq, D_MODEL), lambda i, *_: (layer, 0, sec))
        z, zt = _inproj_call(layer, xs, norm_g3, mod, (smod(1), smod(0)), w_main, w_tail, nseq)
        sa = state_conv_a[layer].reshape(nseq, 2 * D_CONV)
        sq = state_conv_qkv[layer].reshape(nseq, 9 * D_GDN)
        y, na, nq, ssm_stack = _mixer_sample_call(layer, z, zt, sa, sq, state_ssm, conv_a_w,
                                                  conv_qkv_half, alog_rows, dtb_rows, onorm3,
                                                  ssm_stack)
        y_specs = [pl.BlockSpec((nseq, D_CONV), lambda i: (0, 0)),
                   pl.BlockSpec((nseq, D_GDN), lambda i: (0, 1))]
        xs = _outproj_call(layer, y, y, y_specs, xs, mod, smod(2), w_out16, final_g, nseq)
        outs["ca_s"].append(na.reshape(nseq, 2, D_CONV))
        outs["cq_s"].append(nq.reshape(nseq, 3, 3 * D_GDN))

    return (xp.reshape(batch, seq, D_MODEL), xs.reshape(nseq, 1, D_MODEL),
            jnp.stack(outs["ca_p"]), jnp.stack(outs["cq_p"]), jnp.stack(outs["ss_p"]),
            jnp.stack(outs["ca_s"]), jnp.stack(outs["cq_s"]), ssm_stack)
```

```python
import functools
from typing import NamedTuple

import jax
import jax.numpy as jnp
from jax import lax
from jax.experimental import pallas as pl
from jax.experimental.pallas import tpu as pltpu

F32 = jnp.float32
BF16 = jnp.bfloat16

D_MODEL = 2048
DEPTH = 4
D_CONV = 1024
D_GDN = 1024
HEADS = 8
HEAD_DIM = 128
CHUNK = 64
EPS = 1e-6
D_MAIN = 8192
OFF_BA, OFF_CA, OFF_HA, OFF_GA, OFF_QKV, OFF_GB = 0, 1024, 2048, 3072, 4096, 7168
LANES = 128
HALO = 8
VMEM_LIMIT = 56 * 1024 * 1024
VMEM_LIMIT_IN_PROJ = 60 * 1024 * 1024
PROMPT_STREAMS = 2
PREP_TASKS_PER_GAP = (8, 5, 5, 5, 4, 3, 2, 2, 0, 0)


def _silu_of_half(hx):
    return hx + hx * jnp.tanh(hx)


def _silu(x):
    return _silu_of_half(0.5 * x)


def _softplus(x):
    return jnp.maximum(x, 0.0) + jnp.log1p(jnp.exp(-jnp.abs(x)))


def _dot(a, b):
    return jnp.dot(a, b, preferred_element_type=F32)


def _dot_nt(a, b):
    return lax.dot_general(a, b, (((1,), (1,)), ((), ())), preferred_element_type=F32)


def _dot_tn(a, b):
    return lax.dot_general(a, b, (((0,), (0,)), ((), ())), preferred_element_type=F32)


def _dot_hi(a, b):
    return jnp.dot(a, b, preferred_element_type=F32, precision=lax.Precision.HIGHEST)


def _mod_kernel(c_ref, w_ref, b_ref, o_ref):
    s = _silu(c_ref[...]).astype(BF16)
    o_ref[...] = _dot(s, w_ref[...].astype(BF16)) + b_ref[...]


def _mod_call(c_all, w_ada, b_ada):
    rows = c_all.shape[0]
    tn = 1024
    return pl.pallas_call(
        _mod_kernel,
        grid=(DEPTH, 3 * D_MODEL // tn),
        in_specs=[
            pl.BlockSpec((rows, D_MODEL), lambda l, j: (0, 0)),
            pl.BlockSpec((None, D_MODEL, tn), lambda l, j: (l, 0, j)),
            pl.BlockSpec((None, 1, tn), lambda l, j: (l, 0, j)),
        ],
        out_specs=pl.BlockSpec((None, rows, tn), lambda l, j: (l, 0, j)),
        out_shape=jax.ShapeDtypeStruct((DEPTH, rows, 3 * D_MODEL), F32),
        compiler_params=pltpu.CompilerParams(
            dimension_semantics=("arbitrary", "arbitrary"), vmem_limit_bytes=VMEM_LIMIT),
        name="adaln_mod",
    )(c_all, w_ada, b_ada.reshape(DEPTH, 1, 3 * D_MODEL))


def _cast_kernel(w_ref, o_ref):
    o_ref[...] = w_ref[...].astype(BF16)


def _cast_call(w, tr):
    depth, rows, cols = w.shape
    assert rows % tr == 0
    return pl.pallas_call(
        _cast_kernel,
        grid=(depth, rows // tr),
        in_specs=[pl.BlockSpec((None, tr, cols), lambda l, i: (l, i, 0))],
        out_specs=pl.BlockSpec((None, tr, cols), lambda l, i: (l, i, 0)),
        out_shape=jax.ShapeDtypeStruct(w.shape, BF16),
        compiler_params=pltpu.CompilerParams(
            dimension_semantics=("arbitrary", "arbitrary"), vmem_limit_bytes=VMEM_LIMIT),
        name="cast_bf16",
    )(w)


def _inproj_kernel(x_ref, g_ref, sc_ref, sh_ref, w_ref, wt_ref, z_ref, zt_ref, h_scr):
    @pl.when(pl.program_id(1) == 0)
    def _():
        x = x_ref[...]
        y = x * lax.rsqrt(jnp.mean(x * x, axis=-1, keepdims=True) + EPS) * g_ref[...]
        h = (y * (1.0 + sc_ref[...]) + sh_ref[...]).astype(BF16)
        h_scr[...] = h
        zt = _dot_nt(h, wt_ref[...])
        zt_ref[...] = jnp.concatenate(
            [zt, jnp.zeros((zt.shape[0], LANES - zt.shape[1]), F32)], axis=1)

    z = _dot_nt(h_scr[...], w_ref[...])
    sections = z_ref.shape[1] // D_CONV
    for s in range(sections):
        sl = slice(s * D_CONV, (s + 1) * D_CONV)
        is_gate = (pl.program_id(1) * sections + s) % 4 == 3
        z_ref[:, sl] = jnp.where(is_gate, _silu(z[:, sl]), z[:, sl])


def _inproj_call(layer, x, norm_g3, mod_arr, mod_specs, w_main, w_tail, tm):
    rows = x.shape[0]
    tn = 2 * D_CONV if tm >= 1024 else D_CONV
    sc_spec, sh_spec = mod_specs
    return pl.pallas_call(
        _inproj_kernel,
        grid=(rows // tm, D_MAIN // tn),
        in_specs=[
            pl.BlockSpec((tm, D_MODEL), lambda i, j: (i, 0)),
            pl.BlockSpec((None, 1, D_MODEL), lambda i, j: (layer, 0, 0)),
            sc_spec,
            sh_spec,
            pl.BlockSpec((None, tn, D_MODEL), lambda i, j: (layer, j, 0)),
            pl.BlockSpec((None, 2 * HEADS, D_MODEL), lambda i, j: (layer, D_MAIN // (2 * HEADS), 0)),
        ],
        out_specs=[
            pl.BlockSpec((tm, tn), lambda i, j: (i, j)),
            pl.BlockSpec((tm, LANES), lambda i, j: (i, 0)),
        ],
        out_shape=[
            jax.ShapeDtypeStruct((rows, D_MAIN), F32),
            jax.ShapeDtypeStruct((rows, LANES), F32),
        ],
        scratch_shapes=[pltpu.VMEM((tm, D_MODEL), BF16)],
        compiler_params=pltpu.CompilerParams(
            dimension_semantics=("arbitrary", "arbitrary"), vmem_limit_bytes=VMEM_LIMIT_IN_PROJ),
        name="in_proj",
    )(x, norm_g3, mod_arr, mod_arr, w_main, w_tail)


def _outproj_kernel(ya_ref, yb_ref, x_ref, gate_ref, wa_ref, wb_ref, fg_ref, o_ref, *, final):
    acc = (_dot(ya_ref[...].astype(BF16), wa_ref[...])
           + _dot(yb_ref[...].astype(BF16), wb_ref[...]))
    xn = x_ref[...] + gate_ref[...] * acc
    if final:
        xn = xn * lax.rsqrt(jnp.mean(xn * xn, axis=-1, keepdims=True) + EPS) * fg_ref[...]
    o_ref[...] = xn


def _outproj_call(layer, ya, yb, y_specs, x, mod_arr, gate_spec, w_out, final_g, tm):
    rows = x.shape[0]
    half = D_MODEL // 2
    return pl.pallas_call(
        functools.partial(_outproj_kernel, final=(layer == DEPTH - 1)),
        grid=(rows // tm,),
        in_specs=[
            *y_specs,
            pl.BlockSpec((tm, D_MODEL), lambda i: (i, 0)),
            gate_spec,
            pl.BlockSpec((None, half, D_MODEL), lambda i: (layer, 0, 0)),
            pl.BlockSpec((None, half, D_MODEL), lambda i: (layer, 1, 0)),
            pl.BlockSpec((1, D_MODEL), lambda i: (0, 0)),
        ],
        out_specs=pl.BlockSpec((tm, D_MODEL), lambda i: (i, 0)),
        out_shape=jax.ShapeDtypeStruct((rows, D_MODEL), F32),
        compiler_params=pltpu.CompilerParams(
            dimension_semantics=("arbitrary",), vmem_limit_bytes=VMEM_LIMIT),
        name="out_proj",
    )(ya, yb, x, mod_arr, w_out, w_out, final_g)


def _gate_terms(tail, alog_row, dtb_row):
    beta = jax.nn.sigmoid(tail)
    g = -jnp.exp(alog_row) * _softplus(tail + dtb_row)
    return beta, g


def _bdot(a, b):
    return lax.dot_general(a, b, (((2,), (1,)), ((0,), (0,))), preferred_element_type=F32)


def _bdot_nt(a, b):
    return lax.dot_general(a, b, (((2,), (2,)), ((0,), (0,))), preferred_element_type=F32)


def _bdot_tn(a, b):
    return lax.dot_general(a, b, (((1,), (1,)), ((0,), (0,))), preferred_element_type=F32)


def _split_bf16(a):
    hi = a.astype(BF16)
    lo = (a - hi.astype(F32)).astype(BF16)
    return hi, lo


def _unit_lower_inverse_levels(lmat, eye):
    heads, c, _ = lmat.shape
    eye_b = jnp.broadcast_to(eye, lmat.shape)
    pass_x = jnp.concatenate([eye_b, jnp.zeros_like(eye_b)], axis=2).astype(BF16)
    zeros_top = jnp.zeros((heads, c, 2 * c), BF16)
    w = jnp.concatenate([eye_b, -lmat], axis=2)
    for _ in range(6):
        w_hi, w_lo = _split_bf16(w)
        lhs = jnp.concatenate([w_hi, w_lo, w_hi], axis=2)
        rhs = jnp.concatenate([pass_x, w_hi, pass_x, w_hi, zeros_top, w_lo], axis=1)
        w = _bdot(lhs, rhs)
        yield w[:, :, :c], w


class _ChunkOperands(NamedTuple):
    k16: object
    kb16: object
    q16: object
    rhs16: object
    qeg16: object
    kt16: object
    dec: object
    sdec: object
    gate: object


def _operand_scratch(ns):
    nh = ns * HEADS
    return _ChunkOperands(
        k16=pltpu.VMEM((nh, CHUNK, HEAD_DIM), BF16),
        kb16=pltpu.VMEM((nh, CHUNK, HEAD_DIM), BF16),
        q16=pltpu.VMEM((nh, CHUNK, HEAD_DIM), BF16),
        rhs16=pltpu.VMEM((nh, CHUNK, 2 * HEAD_DIM), BF16),
        qeg16=pltpu.VMEM((nh, CHUNK, HEAD_DIM), BF16),
        kt16=pltpu.VMEM((nh, CHUNK, HEAD_DIM), BF16),
        dec=pltpu.VMEM((nh, CHUNK, CHUNK), F32),
        sdec=pltpu.VMEM((nh, 1, HEAD_DIM), F32),
        gate=pltpu.VMEM((ns, CHUNK, D_GDN), F32),
    )


def _prep_tasks(z_ref, zt_ref, wa_ref, wq_ref, alog_ref, dtb_ref, ya_ref, ca_ref, cq_ref,
                buf_a, buf_q, out):
    C = CHUNK
    ns = z_ref.shape[0]
    row = lax.broadcasted_iota(jnp.int32, (C, C), 0)
    col = lax.broadcasted_iota(jnp.int32, (C, C), 1)
    causal = row >= col
    tri = causal.astype(F32)
    sub = lax.broadcasted_iota(jnp.int32, (C // 8, 8, HEAD_DIM), 1)
    never = pl.program_id(1) < 0

    def pinned(row_vec, token):
        return row_vec if token is None else jnp.where(never, token, row_vec)

    def shifted(x3, d):
        r = pltpu.roll(x3, d, axis=1)
        return jnp.where(sub < d, r[:-1], r[1:]).reshape(C, x3.shape[-1])

    def haloed(buf, s, sl):
        x = buf[s, :, sl]
        return x, x.reshape((HALO + C) // 8, 8, HEAD_DIM)

    gates = {}

    def gate_task(s):
        def run(token):
            del token
            beta_s, g_s = _gate_terms(zt_ref[s], alog_ref[...], dtb_ref[...])
            gc_s = _dot_hi(tri, g_s)
            gates[s] = (beta_s, gc_s, gc_s.T)
            buf_a[s, HALO:HALO + C, :] = (z_ref[s, :, OFF_CA:OFF_CA + D_CONV]
                                          * z_ref[s, :, OFF_HA:OFF_HA + D_CONV])
            ca_ref[s] = buf_a[s, HALO + C - 2:HALO + C, :]
            buf_q[s, HALO:HALO + C, :] = z_ref[s, :, OFF_QKV:OFF_QKV + 3 * D_GDN]
            cq_ref[s] = buf_q[s, HALO + C - 3:HALO + C, :]
        return run

    def conv_a_task(s, off):
        def run(token):
            sl = slice(off, off + HEAD_DIM)
            xa, xa3 = haloed(buf_a, s, sl)
            conv = (shifted(xa3, 2) * pinned(wa_ref[0:1, sl], token)
                    + shifted(xa3, 1) * pinned(wa_ref[1:2, sl], token)
                    + xa[HALO:] * pinned(wa_ref[2:3, sl], token))
            ya = (z_ref[s, :, OFF_BA + off:OFF_BA + off + HEAD_DIM] * conv
                  * z_ref[s, :, OFF_GA + off:OFF_GA + off + HEAD_DIM])
            ya_ref[s, :, sl] = ya.astype(ya_ref.dtype)
            buf_a[s, 0:HALO, sl] = xa[C:]
        return run

    def conv_qkv(s, off, token):
        sl = slice(off, off + HEAD_DIM)
        x, x3 = haloed(buf_q, s, sl)
        acc = x[HALO:] * pinned(wq_ref[3:4, sl], token)
        for d in range(1, 4):
            acc = acc + shifted(x3, d) * pinned(wq_ref[3 - d:4 - d, sl], token)
        buf_q[s, 0:HALO, sl] = x[C:]
        return _silu_of_half(acc)

    def head_task(s, h):
        def run(token):
            n = s * HEADS + h
            beta_s, gc_s, gc_t = gates[s]
            sl = slice(h * HEAD_DIM, (h + 1) * HEAD_DIM)
            q = conv_qkv(s, h * HEAD_DIM, token)
            k = conv_qkv(s, D_GDN + h * HEAD_DIM, token)
            v = conv_qkv(s, 2 * D_GDN + h * HEAD_DIM, token)
            q = q * (lax.rsqrt(jnp.sum(q * q, axis=-1, keepdims=True) + EPS) * (HEAD_DIM ** -0.5))
            k = k * lax.rsqrt(jnp.sum(k * k, axis=-1, keepdims=True) + EPS)
            beta = jnp.broadcast_to(beta_s[:, h:h + 1], (C, HEAD_DIM))
            gcol = jnp.broadcast_to(gc_s[:, HEADS + h:HEADS + h + 1], (C, HEAD_DIM))
            grow = gc_t[HEADS + h:HEADS + h + 1, :]
            glast = gcol[C - 1:C, :]
            eg = jnp.exp(gcol)
            kb = k * beta
            out.k16[n] = k.astype(BF16)
            out.kb16[n] = kb.astype(BF16)
            out.q16[n] = q.astype(BF16)
            out.rhs16[n] = jnp.concatenate([v * beta, kb * eg], axis=1).astype(BF16)
            out.qeg16[n] = (q * eg).astype(BF16)
            out.kt16[n] = (k * jnp.exp(glast - gcol)).astype(BF16)
            out.dec[n] = jnp.exp(jnp.where(causal, gcol[:, :C] - grow, -jnp.inf))
            out.sdec[n] = jnp.exp(glast)
            out.gate[s, :, sl] = z_ref[s, :, OFF_GB + h * HEAD_DIM:OFF_GB + (h + 1) * HEAD_DIM]
        return run

    tasks = [gate_task(s) for s in range(ns)]
    for s in range(ns):
        for h in range(HEADS):
            tasks.append(head_task(s, h))
            tasks.append(conv_a_task(s, h * HEAD_DIM))
    return tasks


def _scan_stages(ops, on_ref, yb_ref, s_scr):
    C = CHUNK
    ns = yb_ref.shape[0]
    row = lax.broadcasted_iota(jnp.int32, (C, C), 0)
    col = lax.broadcasted_iota(jnp.int32, (C, C), 1)
    strict = (row > col)[None]
    eye = (row == col).astype(F32)[None]

    k16 = ops.k16[...]
    dec_causal = ops.dec[...]
    lmat = _bdot_nt(ops.kb16[...], k16) * jnp.where(strict, dec_causal, 0.0)
    attn = (_bdot_nt(ops.q16[...], k16) * dec_causal).astype(BF16)
    nh = k16.shape[0]
    token_of = lambda a: a[nh - 1, C - 1:C, :HEAD_DIM]
    yield None
    for tmat, w_level in _unit_lower_inverse_levels(lmat, eye):
        yield token_of(w_level)
    uw = _bdot(tmat.astype(BF16), ops.rhs16[...])
    yield token_of(uw)
    s_old = s_scr[...]
    lhs = jnp.concatenate([uw[:, :, HEAD_DIM:].astype(BF16), ops.qeg16[...]], axis=1)
    ws = _bdot(lhs, s_old.astype(BF16))
    yield token_of(ws)
    u = uw[:, :, :HEAD_DIM] - ws[:, :C]
    u16 = u.astype(BF16)
    o = ws[:, C:] + _bdot(attn, u16)
    s_scr[...] = s_old * ops.sdec[...] + _bdot_tn(ops.kt16[...], u16)
    yield token_of(o)
    o = o * lax.rsqrt(jnp.mean(o * o, axis=-1, keepdims=True) + EPS) * on_ref[...]
    for s in range(ns):
        for h in range(HEADS):
            sl = slice(h * HEAD_DIM, (h + 1) * HEAD_DIM)
            yb_ref[s, :, sl] = (o[s * HEADS + h] * ops.gate[s, :, sl]).astype(yb_ref.dtype)


def _interleave(stages, tasks, per_gap):
    todo = list(tasks)
    token = None
    for n, token in zip(list(per_gap) + [0] * 64, stages):
        for task in todo[:n]:
            task(token)
        todo = todo[n:]
    for task in todo:
        task(token)


def _mixer_prompt_kernel(z_ref, zt_ref, wa_ref, wq_ref, alog_ref, dtb_ref, on_ref,
                         ya_ref, yb_ref, ca_ref, cq_ref, ss_ref, buf_a, buf_q, s_scr, *operand_refs):
    t = pl.program_id(1)
    ns = z_ref.shape[0]
    n_fields = len(_ChunkOperands._fields)
    sets = (_ChunkOperands(*operand_refs[:n_fields]), _ChunkOperands(*operand_refs[n_fields:]))

    @pl.when(t == 0)
    def _():
        buf_a[:, 0:HALO, :] = jnp.zeros((ns, HALO, D_CONV), F32)
        buf_q[:, 0:HALO, :] = jnp.zeros((ns, HALO, 3 * D_GDN), F32)
        s_scr[...] = jnp.zeros(s_scr.shape, F32)
        for ref in sets[1]:
            ref[...] = jnp.zeros(ref.shape, ref.dtype)

    for parity in range(2):
        @pl.when(t % 2 == parity)
        def _(parity=parity):
            _interleave(
                _scan_stages(sets[1 - parity], on_ref, yb_ref, s_scr),
                _prep_tasks(z_ref, zt_ref, wa_ref, wq_ref, alog_ref, dtb_ref, ya_ref, ca_ref,
                            cq_ref, buf_a, buf_q, sets[parity]),
                PREP_TASKS_PER_GAP)

    @pl.when(t == pl.num_programs(1) - 1)
    def _():
        ss_ref[...] = s_scr[...].reshape(ss_ref.shape)


def _mixer_prompt_call(layer, z, zt, conv_a_w, conv_qkv_w, alog_rows, dtb_rows, onorm3, batch, seq):
    nc = seq // CHUNK
    ns = PROMPT_STREAMS
    lsel = lambda g, t: (layer, 0, 0)
    cur = lambda g, t: (g, jnp.minimum(t, nc - 1), 0)
    prev = lambda g, t: (g, jnp.maximum(t - 1, 0), 0)
    scratch = _operand_scratch(ns)
    return pl.pallas_call(
        _mixer_prompt_kernel,
        grid=(batch // ns, nc + 1),
        in_specs=[
            pl.BlockSpec((ns, CHUNK, D_MAIN), cur),
            pl.BlockSpec((ns, CHUNK, LANES), cur),
            pl.BlockSpec((None, 3, D_CONV), lsel),
            pl.BlockSpec((None, 4, 3 * D_GDN), lsel),
            pl.BlockSpec((None, 1, LANES), lsel),
            pl.BlockSpec((None, 1, LANES), lsel),
            pl.BlockSpec((None, 1, HEAD_DIM), lsel),
        ],
        out_specs=[
            pl.BlockSpec((ns, CHUNK, D_CONV), lambda g, t: (g, t, 0)),
            pl.BlockSpec((ns, CHUNK, D_GDN), prev),
            pl.BlockSpec((ns, 2, D_CONV), lambda g, t: (g, 0, 0)),
            pl.BlockSpec((ns, 3, 3 * D_GDN), lambda g, t: (g, 0, 0)),
            pl.BlockSpec((ns, HEADS, HEAD_DIM, HEAD_DIM), lambda g, t: (g, 0, 0, 0)),
        ],
        out_shape=[
            jax.ShapeDtypeStruct((batch, seq + CHUNK, D_CONV), BF16),
            jax.ShapeDtypeStruct((batch, seq, D_GDN), BF16),
            jax.ShapeDtypeStruct((batch, 2, D_CONV), F32),
            jax.ShapeDtypeStruct((batch, 3, 3 * D_GDN), F32),
            jax.ShapeDtypeStruct((batch, HEADS, HEAD_DIM, HEAD_DIM), F32),
        ],
        scratch_shapes=[
            pltpu.VMEM((ns, HALO + CHUNK, D_CONV), F32),
            pltpu.VMEM((ns, HALO + CHUNK, 3 * D_GDN), F32),
            pltpu.VMEM((ns * HEADS, HEAD_DIM, HEAD_DIM), F32),
            *scratch, *scratch,
        ],
        compiler_params=pltpu.CompilerParams(
            dimension_semantics=("arbitrary", "arbitrary"), vmem_limit_bytes=VMEM_LIMIT),
        name="mixer_prompt",
    )(z.reshape(batch, seq, D_MAIN), zt.reshape(batch, seq, LANES), conv_a_w, conv_qkv_w,
      alog_rows, dtb_rows, onorm3)


def _mixer_sample_kernel(z_ref, zt_ref, sa_ref, sq_ref, s_ref, wa_ref, wq_ref, alog_ref, dtb_ref,
                         on_ref, y_ref, na_ref, nq_ref, ns_ref):
    nb = z_ref.shape[0]

    ch = z_ref[:, OFF_CA:OFF_CA + D_CONV] * z_ref[:, OFF_HA:OFF_HA + D_CONV]
    prev0 = sa_ref[:, 0:D_CONV]
    prev1 = sa_ref[:, D_CONV:2 * D_CONV]
    conv = prev0 * wa_ref[0:1, :] + prev1 * wa_ref[1:2, :] + ch * wa_ref[2:3, :]
    y_ref[:, 0:D_CONV] = (z_ref[:, OFF_BA:OFF_BA + D_CONV] * conv
                          * z_ref[:, OFF_GA:OFF_GA + D_CONV])
    na_ref[:, 0:D_CONV] = prev1
    na_ref[:, D_CONV:2 * D_CONV] = ch

    w3 = 3 * D_GDN
    nq_ref[:, 0:w3] = sq_ref[:, w3:2 * w3]
    nq_ref[:, w3:2 * w3] = sq_ref[:, 2 * w3:3 * w3]
    nq_ref[:, 2 * w3:3 * w3] = z_ref[:, OFF_QKV:OFF_QKV + w3]

    beta_all, g_all = _gate_terms(zt_ref[...], alog_ref[...], dtb_ref[...])
    eg_all = jnp.exp(g_all)
    rowid = lax.broadcasted_iota(jnp.int32, (nb, HEAD_DIM), 0)

    def conv_qkv(off):
        sl = slice(off, off + HEAD_DIM)
        acc = z_ref[:, OFF_QKV + off:OFF_QKV + off + HEAD_DIM] * wq_ref[3:4, sl]
        for j in range(3):
            acc = acc + sq_ref[:, j * w3 + off:j * w3 + off + HEAD_DIM] * wq_ref[j:j + 1, sl]
        return _silu_of_half(acc)

    for h in range(HEADS):
        q = conv_qkv(h * HEAD_DIM)
        k = conv_qkv(D_GDN + h * HEAD_DIM)
        v = conv_qkv(2 * D_GDN + h * HEAD_DIM)
        q = q * (lax.rsqrt(jnp.sum(q * q, axis=-1, keepdims=True) + EPS) * (HEAD_DIM ** -0.5))
        k = k * lax.rsqrt(jnp.sum(k * k, axis=-1, keepdims=True) + EPS)
        beta = beta_all[:, h:h + 1]
        eg = eg_all[:, HEADS + h:HEADS + h + 1]
        kq = jnp.concatenate([k, q], axis=0)
        ks = jnp.zeros((nb, HEAD_DIM), F32)
        qs = jnp.zeros((nb, HEAD_DIM), F32)
        for j in range(nb):
            r = _dot(kq, s_ref[j, h])
            ks = jnp.where(rowid == j, r[:nb], ks)
            qs = jnp.where(rowid == j, r[nb:], qs)
        u = beta * (v - eg * ks)
        o = eg * qs + jnp.sum(q * k, axis=-1, keepdims=True) * u
        for j in range(nb):
            outer = _dot_tn(jnp.where(rowid == j, k, 0.0), u)
            ns_ref[j, h] = s_ref[j, h] * eg[j:j + 1, :] + outer
        o = o * lax.rsqrt(jnp.mean(o * o, axis=-1, keepdims=True) + EPS) * on_ref[...]
        gb = z_ref[:, OFF_GB + h * HEAD_DIM:OFF_GB + (h + 1) * HEAD_DIM]
        y_ref[:, D_CONV + h * HEAD_DIM:D_CONV + (h + 1) * HEAD_DIM] = o * gb


def _mixer_sample_chained_kernel(prev_ref, *refs):
    del prev_ref
    _mixer_sample_kernel(*refs)


def _mixer_sample_call(layer, z, zt, sa, sq, state_ssm, conv_a_w, conv_qkv_w, alog_rows, dtb_rows,
                       onorm3, ssm_stack):
    nseq = z.shape[0]
    nb = 8
    lsel = lambda i: (layer, 0, 0)
    chained = ssm_stack is not None
    return pl.pallas_call(
        _mixer_sample_chained_kernel if chained else _mixer_sample_kernel,
        grid=(nseq // nb,),
        input_output_aliases={0: 3} if chained else {},
        in_specs=([pl.BlockSpec(memory_space=pl.ANY)] if chained else []) + [
            pl.BlockSpec((nb, D_MAIN), lambda i: (i, 0)),
            pl.BlockSpec((nb, LANES), lambda i: (i, 0)),
            pl.BlockSpec((nb, 2 * D_CONV), lambda i: (i, 0)),
            pl.BlockSpec((nb, 9 * D_GDN), lambda i: (i, 0)),
            pl.BlockSpec((None, nb, HEADS, HEAD_DIM, HEAD_DIM), lambda i: (layer, i, 0, 0, 0)),
            pl.BlockSpec((None, 3, D_CONV), lsel),
            pl.BlockSpec((None, 4, 3 * D_GDN), lsel),
            pl.BlockSpec((None, 1, LANES), lsel),
            pl.BlockSpec((None, 1, LANES), lsel),
            pl.BlockSpec((None, 1, HEAD_DIM), lsel),
        ],
        out_specs=[
            pl.BlockSpec((nb, D_MODEL), lambda i: (i, 0)),
            pl.BlockSpec((nb, 2 * D_CONV), lambda i: (i, 0)),
            pl.BlockSpec((nb, 9 * D_GDN), lambda i: (i, 0)),
            pl.BlockSpec((None, nb, HEADS, HEAD_DIM, HEAD_DIM), lambda i: (layer, i, 0, 0, 0)),
        ],
        out_shape=[
            jax.ShapeDtypeStruct((nseq, D_MODEL), F32),
            jax.ShapeDtypeStruct((nseq, 2 * D_CONV), F32),
            jax.ShapeDtypeStruct((nseq, 9 * D_GDN), F32),
            jax.ShapeDtypeStruct((DEPTH, nseq, HEADS, HEAD_DIM, HEAD_DIM), F32),
        ],
        compiler_params=pltpu.CompilerParams(
            dimension_semantics=("arbitrary",), vmem_limit_bytes=VMEM_LIMIT),
        name="mixer_sample",
    )(*([ssm_stack] if chained else []), z, zt, sa, sq, state_ssm, conv_a_w, conv_qkv_w,
      alog_rows, dtb_rows, onorm3)


def kernel(x_prompt, x_sample, state_conv_a, state_conv_qkv, state_ssm, c_prompt, c_sample, norm_g, w_ada, b_ada, w_in, conv_a_w, conv_qkv_w, a_log, dt_bias, o_norm_g, w_out, final_norm_g):
    batch, seq, _ = x_prompt.shape
    nseq = x_sample.shape[0]
    assert x_sample.shape[1] == 1 and seq % CHUNK == 0

    d_in_proj = w_in.shape[2]
    w_main = _cast_call(jnp.swapaxes(w_in, 1, 2), d_in_proj // 9)
    w_tail = w_main
    w_out16 = w_out.astype(BF16)
    norm_g3 = norm_g.reshape(DEPTH, 1, D_MODEL)
    onorm3 = o_norm_g.reshape(DEPTH, 1, HEAD_DIM)
    final_g = final_norm_g.reshape(1, D_MODEL)
    conv_qkv_half = 0.5 * conv_qkv_w
    pad_heads = ((0, 0), (HEADS, LANES - 2 * HEADS))
    alog_rows = jnp.pad(a_log, pad_heads).reshape(DEPTH, 1, LANES)
    dtb_rows = jnp.pad(dt_bias, pad_heads).reshape(DEPTH, 1, LANES)

    n_cond = nseq + batch
    n_cond_pad = -(-n_cond // 8) * 8
    c_all = jnp.concatenate(
        [c_sample, c_prompt, jnp.zeros((n_cond_pad - n_cond, D_MODEL), F32)], axis=0)
    mod = _mod_call(c_all, w_ada, b_ada)
    mod4 = mod.reshape(DEPTH, n_cond_pad, 1, 3 * D_MODEL)

    xp = x_prompt.reshape(batch * seq, D_MODEL)
    xs = x_sample.reshape(nseq, D_MODEL)
    tm_in = 1024
    tm_out = 512
    outs = {k: [] for k in ("ca_p", "cq_p", "ss_p", "ca_s", "cq_s")}
    ssm_stack = None

    for layer in range(DEPTH):
        def pmod(sec, per_batch, layer=layer):
            return pl.BlockSpec((None, None, 1, D_MODEL),
                                lambda i, *_: (layer, nseq + i // per_batch, 0, sec))
        z, zt = _inproj_call(layer, xp, norm_g3, mod4,
                             (pmod(1, seq // tm_in), pmod(0, seq // tm_in)), w_main, w_tail, tm_in)
        ya, yb, ca, cq, ss = _mixer_prompt_call(layer, z, zt, conv_a_w, conv_qkv_half, alog_rows,
                                                dtb_rows, onorm3, batch, seq)
        per_seq = seq // tm_out
        y_specs = [pl.BlockSpec((None, tm_out, D_CONV), lambda i: (i // per_seq, i % per_seq, 0)),
                   pl.BlockSpec((None, tm_out, D_GDN), lambda i: (i // per_seq, i % per_seq, 0))]
        xp = _outproj_call(layer, ya, yb, y_specs, xp, mod4, pmod(2, per_seq), w_out16, final_g,
                           tm_out)
        outs["ca_p"].append(ca)
        outs["cq_p"].append(cq)
        outs["ss_p"].append(ss)

        def smod(sec, layer=layer):
            return pl.BlockSpec((None, nseq, D_MODEL), lambda i, *_: (layer, 0, sec))
        z, zt = _inproj_call(layer, xs, norm_g3, mod, (smod(1), smod(0)), w_main, w_tail, nseq)
        sa = state_conv_a[layer].reshape(nseq, 2 * D_CONV)
        sq = state_conv_qkv[layer].reshape(nseq, 9 * D_GDN)
        y, na, nq, ssm_stack = _mixer_sample_call(layer, z, zt, sa, sq, state_ssm, conv_a_w,
                                                  conv_qkv_half, alog_rows, dtb_rows, onorm3,
                                                  ssm_stack)
        y_specs = [pl.BlockSpec((nseq, D_CONV), lambda i: (0, 0)),
                   pl.BlockSpec((nseq, D_GDN), lambda i: (0, 1))]
        xs = _outproj_call(layer, y, y, y_specs, xs, mod, smod(2), w_out16, final_g, nseq)
        outs["ca_s"].append(na.reshape(nseq, 2, D_CONV))
        outs["cq_s"].append(nq.reshape(nseq, 3, 3 * D_GDN))

    return (xp.reshape(batch, seq, D_MODEL), xs.reshape(nseq, 1, D_MODEL),
            jnp.stack(outs["ca_p"]), jnp.stack(outs["cq_p"]), jnp.stack(outs["ss_p"]),
            jnp.stack(outs["ca_s"]), jnp.stack(outs["cq_s"]), ssm_stack)
```

```python
import functools
from typing import NamedTuple

import jax
import jax.numpy as jnp
from jax import lax
from jax.experimental import pallas as pl
from jax.experimental.pallas import tpu as pltpu

F32 = jnp.float32
BF16 = jnp.bfloat16

D_MODEL = 2048
DEPTH = 4
D_CONV = 1024
D_GDN = 1024
HEADS = 8
HEAD_DIM = 128
CHUNK = 64
EPS = 1e-6
D_MAIN = 8192
OFF_BA, OFF_CA, OFF_HA, OFF_GA, OFF_QKV, OFF_GB = 0, 1024, 2048, 3072, 4096, 7168
LANES = 128
HALO = 8
VMEM_LIMIT = 56 * 1024 * 1024
VMEM_LIMIT_IN_PROJ = 60 * 1024 * 1024
PROMPT_STREAMS = 2
PREP_TASKS_PER_GAP = (8, 5, 5, 5, 4, 3, 2, 2, 0, 0)


def _silu_of_half(hx):
    return hx + hx * jnp.tanh(hx)


def _silu(x):
    return _silu_of_half(0.5 * x)


def _softplus(x):
    return jnp.maximum(x, 0.0) + jnp.log1p(jnp.exp(-jnp.abs(x)))


def _dot(a, b):
    return jnp.dot(a, b, preferred_element_type=F32)


def _dot_nt(a, b):
    return lax.dot_general(a, b, (((1,), (1,)), ((), ())), preferred_element_type=F32)


def _dot_tn(a, b):
    return lax.dot_general(a, b, (((0,), (0,)), ((), ())), preferred_element_type=F32)


def _dot_hi(a, b):
    return jnp.dot(a, b, preferred_element_type=F32, precision=lax.Precision.HIGHEST)


def _mod_kernel(c_ref, w_ref, b_ref, o_ref):
    s = _silu(c_ref[...]).astype(BF16)
    o_ref[...] = _dot(s, w_ref[...].astype(BF16)) + b_ref[...]


def _mod_call(c_all, w_ada, b_ada):
    rows = c_all.shape[0]
    tn = 1024
    return pl.pallas_call(
        _mod_kernel,
        grid=(DEPTH, 3 * D_MODEL // tn),
        in_specs=[
            pl.BlockSpec((rows, D_MODEL), lambda l, j: (0, 0)),
            pl.BlockSpec((None, D_MODEL, tn), lambda l, j: (l, 0, j)),
            pl.BlockSpec((None, 1, tn), lambda l, j: (l, 0, j)),
        ],
        out_specs=pl.BlockSpec((None, rows, tn), lambda l, j: (l, 0, j)),
        out_shape=jax.ShapeDtypeStruct((DEPTH, rows, 3 * D_MODEL), F32),
        compiler_params=pltpu.CompilerParams(
            dimension_semantics=("arbitrary", "arbitrary"), vmem_limit_bytes=VMEM_LIMIT),
        name="adaln_mod",
    )(c_all, w_ada, b_ada.reshape(DEPTH, 1, 3 * D_MODEL))


def _cast_kernel(w_ref, o_ref):
    o_ref[...] = w_ref[...].astype(BF16)


def _cast_call(w, tr):
    depth, rows, cols = w.shape
    assert rows % tr == 0
    return pl.pallas_call(
        _cast_kernel,
        grid=(depth, rows // tr),
        in_specs=[pl.BlockSpec((None, tr, cols), lambda l, i: (l, i, 0))],
        out_specs=pl.BlockSpec((None, tr, cols), lambda l, i: (l, i, 0)),
        out_shape=jax.ShapeDtypeStruct(w.shape, BF16),
        compiler_params=pltpu.CompilerParams(
            dimension_semantics=("arbitrary", "arbitrary"), vmem_limit_bytes=VMEM_LIMIT),
        name="cast_bf16",
    )(w)


def _cond_rows(ref, cond_row):
    if cond_row is None:
        return ref[...]
    return ref[pl.ds(cond_row(pl.program_id(0)), 1), :]


def _inproj_kernel(x_ref, g_ref, sc_ref, sh_ref, w_ref, wt_ref, z_ref, zt_ref, h_scr, *, cond_row):
    @pl.when(pl.program_id(1) == 0)
    def _():
        x = x_ref[...]
        y = x * lax.rsqrt(jnp.mean(x * x, axis=-1, keepdims=True) + EPS) * g_ref[...]
        h = (y * (1.0 + _cond_rows(sc_ref, cond_row)) + _cond_rows(sh_ref, cond_row)).astype(BF16)
        h_scr[...] = h
        zt = _dot_nt(h, wt_ref[...])
        zt_ref[...] = jnp.concatenate(
            [zt, jnp.zeros((zt.shape[0], LANES - zt.shape[1]), F32)], axis=1)

    z = _dot_nt(h_scr[...], w_ref[...])
    sections = z_ref.shape[1] // D_CONV
    for s in range(sections):
        sl = slice(s * D_CONV, (s + 1) * D_CONV)
        is_gate = (pl.program_id(1) * sections + s) % 4 == 3
        z_ref[:, sl] = jnp.where(is_gate, _silu(z[:, sl]), z[:, sl])


def _inproj_call(layer, x, norm_g3, mod_arr, mod_specs, cond_row, w_main, w_tail, tm):
    rows = x.shape[0]
    tn = 2 * D_CONV if tm >= 1024 else D_CONV
    sc_spec, sh_spec = mod_specs
    return pl.pallas_call(
        functools.partial(_inproj_kernel, cond_row=cond_row),
        grid=(rows // tm, D_MAIN // tn),
        in_specs=[
            pl.BlockSpec((tm, D_MODEL), lambda i, j: (i, 0)),
            pl.BlockSpec((None, 1, D_MODEL), lambda i, j: (layer, 0, 0)),
            sc_spec,
            sh_spec,
            pl.BlockSpec((None, tn, D_MODEL), lambda i, j: (layer, j, 0)),
            pl.BlockSpec((None, 2 * HEADS, D_MODEL), lambda i, j: (layer, D_MAIN // (2 * HEADS), 0)),
        ],
        out_specs=[
            pl.BlockSpec((tm, tn), lambda i, j: (i, j)),
            pl.BlockSpec((tm, LANES), lambda i, j: (i, 0)),
        ],
        out_shape=[
            jax.ShapeDtypeStruct((rows, D_MAIN), F32),
            jax.ShapeDtypeStruct((rows, LANES), F32),
        ],
        scratch_shapes=[pltpu.VMEM((tm, D_MODEL), BF16)],
        compiler_params=pltpu.CompilerParams(
            dimension_semantics=("arbitrary", "arbitrary"), vmem_limit_bytes=VMEM_LIMIT_IN_PROJ),
        name="in_proj",
    )(x, norm_g3, mod_arr, mod_arr, w_main, w_tail)


def _outproj_kernel(ya_ref, yb_ref, x_ref, gate_ref, wa_ref, wb_ref, fg_ref, o_ref, *, final,
                    cond_row):
    acc = (_dot(ya_ref[...].astype(BF16), wa_ref[...])
           + _dot(yb_ref[...].astype(BF16), wb_ref[...]))
    xn = x_ref[...] + _cond_rows(gate_ref, cond_row) * acc
    if final:
        xn = xn * lax.rsqrt(jnp.mean(xn * xn, axis=-1, keepdims=True) + EPS) * fg_ref[...]
    o_ref[...] = xn


def _outproj_call(layer, ya, yb, y_specs, x, mod_arr, gate_spec, cond_row, w_out, final_g, tm):
    rows = x.shape[0]
    half = D_MODEL // 2
    return pl.pallas_call(
        functools.partial(_outproj_kernel, final=(layer == DEPTH - 1), cond_row=cond_row),
        grid=(rows // tm,),
        in_specs=[
            *y_specs,
            pl.BlockSpec((tm, D_MODEL), lambda i: (i, 0)),
            gate_spec,
            pl.BlockSpec((None, half, D_MODEL), lambda i: (layer, 0, 0)),
            pl.BlockSpec((None, half, D_MODEL), lambda i: (layer, 1, 0)),
            pl.BlockSpec((1, D_MODEL), lambda i: (0, 0)),
        ],
        out_specs=pl.BlockSpec((tm, D_MODEL), lambda i: (i, 0)),
        out_shape=jax.ShapeDtypeStruct((rows, D_MODEL), F32),
        compiler_params=pltpu.CompilerParams(
            dimension_semantics=("arbitrary",), vmem_limit_bytes=VMEM_LIMIT),
        name="out_proj",
    )(ya, yb, x, mod_arr, w_out, w_out, final_g)


def _gate_terms(tail, alog_row, dtb_row):
    beta = jax.nn.sigmoid(tail)
    g = -jnp.exp(alog_row) * _softplus(tail + dtb_row)
    return beta, g


def _bdot(a, b):
    return lax.dot_general(a, b, (((2,), (1,)), ((0,), (0,))), preferred_element_type=F32)


def _bdot_nt(a, b):
    return lax.dot_general(a, b, (((2,), (2,)), ((0,), (0,))), preferred_element_type=F32)


def _bdot_tn(a, b):
    return lax.dot_general(a, b, (((1,), (1,)), ((0,), (0,))), preferred_element_type=F32)


def _split_bf16(a):
    hi = a.astype(BF16)
    lo = (a - hi.astype(F32)).astype(BF16)
    return hi, lo


def _unit_lower_inverse_levels(lmat, eye):
    heads, c, _ = lmat.shape
    eye_b = jnp.broadcast_to(eye, lmat.shape)
    pass_x = jnp.concatenate([eye_b, jnp.zeros_like(eye_b)], axis=2).astype(BF16)
    zeros_top = jnp.zeros((heads, c, 2 * c), BF16)
    w = jnp.concatenate([eye_b, -lmat], axis=2)
    for _ in range(6):
        w_hi, w_lo = _split_bf16(w)
        lhs = jnp.concatenate([w_hi, w_lo, w_hi], axis=2)
        rhs = jnp.concatenate([pass_x, w_hi, pass_x, w_hi, zeros_top, w_lo], axis=1)
        w = _bdot(lhs, rhs)
        yield w[:, :, :c], w


class _ChunkOperands(NamedTuple):
    k16: object
    kb16: object
    q16: object
    rhs16: object
    qeg16: object
    kt16: object
    dec: object
    sdec: object
    gate: object


def _operand_scratch(ns):
    nh = ns * HEADS
    return _ChunkOperands(
        k16=pltpu.VMEM((nh, CHUNK, HEAD_DIM), BF16),
        kb16=pltpu.VMEM((nh, CHUNK, HEAD_DIM), BF16),
        q16=pltpu.VMEM((nh, CHUNK, HEAD_DIM), BF16),
        rhs16=pltpu.VMEM((nh, CHUNK, 2 * HEAD_DIM), BF16),
        qeg16=pltpu.VMEM((nh, CHUNK, HEAD_DIM), BF16),
        kt16=pltpu.VMEM((nh, CHUNK, HEAD_DIM), BF16),
        dec=pltpu.VMEM((nh, CHUNK, CHUNK), F32),
        sdec=pltpu.VMEM((nh, 1, HEAD_DIM), F32),
        gate=pltpu.VMEM((ns, CHUNK, D_GDN), F32),
    )


def _prep_tasks(z_ref, zt_ref, wa_ref, wq_ref, alog_ref, dtb_ref, ya_ref, ca_ref, cq_ref,
                buf_a, buf_q, out):
    C = CHUNK
    ns = z_ref.shape[0]
    row = lax.broadcasted_iota(jnp.int32, (C, C), 0)
    col = lax.broadcasted_iota(jnp.int32, (C, C), 1)
    causal = row >= col
    tri = causal.astype(F32)
    sub = lax.broadcasted_iota(jnp.int32, (C // 8, 8, HEAD_DIM), 1)
    never = pl.program_id(1) < 0

    def pinned(row_vec, token):
        return row_vec if token is None else jnp.where(never, token, row_vec)

    def shifted(x3, d):
        r = pltpu.roll(x3, d, axis=1)
        return jnp.where(sub < d, r[:-1], r[1:]).reshape(C, x3.shape[-1])

    def haloed(buf, s, sl):
        x = buf[s, :, sl]
        return x, x.reshape((HALO + C) // 8, 8, HEAD_DIM)

    gates = {}

    def gate_task(s):
        def run(token):
            del token
            beta_s, g_s = _gate_terms(zt_ref[s], alog_ref[...], dtb_ref[...])
            gc_s = _dot_hi(tri, g_s)
            gates[s] = (beta_s, gc_s, gc_s.T)
            buf_a[s, HALO:HALO + C, :] = (z_ref[s, :, OFF_CA:OFF_CA + D_CONV]
                                          * z_ref[s, :, OFF_HA:OFF_HA + D_CONV])
            ca_ref[s] = buf_a[s, HALO + C - 2:HALO + C, :]
            buf_q[s, HALO:HALO + C, :] = z_ref[s, :, OFF_QKV:OFF_QKV + 3 * D_GDN]
            cq_ref[s] = buf_q[s, HALO + C - 3:HALO + C, :]
        return run

    def conv_a_task(s, off):
        def run(token):
            sl = slice(off, off + HEAD_DIM)
            xa, xa3 = haloed(buf_a, s, sl)
            conv = (shifted(xa3, 2) * pinned(wa_ref[0:1, sl], token)
                    + shifted(xa3, 1) * pinned(wa_ref[1:2, sl], token)
                    + xa[HALO:] * pinned(wa_ref[2:3, sl], token))
            ya = (z_ref[s, :, OFF_BA + off:OFF_BA + off + HEAD_DIM] * conv
                  * z_ref[s, :, OFF_GA + off:OFF_GA + off + HEAD_DIM])
            ya_ref[s, :, sl] = ya.astype(ya_ref.dtype)
            buf_a[s, 0:HALO, sl] = xa[C:]
        return run

    def conv_qkv(s, off, token):
        sl = slice(off, off + HEAD_DIM)
        x, x3 = haloed(buf_q, s, sl)
        acc = x[HALO:] * pinned(wq_ref[3:4, sl], token)
        for d in range(1, 4):
            acc = acc + shifted(x3, d) * pinned(wq_ref[3 - d:4 - d, sl], token)
        buf_q[s, 0:HALO, sl] = x[C:]
        return _silu_of_half(acc)

    def head_task(s, h):
        def run(token):
            n = s * HEADS + h
            beta_s, gc_s, gc_t = gates[s]
            sl = slice(h * HEAD_DIM, (h + 1) * HEAD_DIM)
            q = conv_qkv(s, h * HEAD_DIM, token)
            k = conv_qkv(s, D_GDN + h * HEAD_DIM, token)
            v = conv_qkv(s, 2 * D_GDN + h * HEAD_DIM, token)
            q = q * (lax.rsqrt(jnp.sum(q * q, axis=-1, keepdims=True) + EPS) * (HEAD_DIM ** -0.5))
            k = k * lax.rsqrt(jnp.sum(k * k, axis=-1, keepdims=True) + EPS)
            beta = jnp.broadcast_to(beta_s[:, h:h + 1], (C, HEAD_DIM))
            gcol = jnp.broadcast_to(gc_s[:, HEADS + h:HEADS + h + 1], (C, HEAD_DIM))
            grow = gc_t[HEADS + h:HEADS + h + 1, :]
            glast = gcol[C - 1:C, :]
            eg = jnp.exp(gcol)
            kb = k * beta
            out.k16[n] = k.astype(BF16)
            out.kb16[n] = kb.astype(BF16)
            out.q16[n] = q.astype(BF16)
            out.rhs16[n] = jnp.concatenate([v * beta, kb * eg], axis=1).astype(BF16)
            out.qeg16[n] = (q * eg).astype(BF16)
            out.kt16[n] = (k * jnp.exp(glast - gcol)).astype(BF16)
            out.dec[n] = jnp.exp(jnp.where(causal, gcol[:, :C] - grow, -jnp.inf))
            out.sdec[n] = jnp.exp(glast)
            out.gate[s, :, sl] = z_ref[s, :, OFF_GB + h * HEAD_DIM:OFF_GB + (h + 1) * HEAD_DIM]
        return run

    tasks = [gate_task(s) for s in range(ns)]
    for s in range(ns):
        for h in range(HEADS):
            tasks.append(head_task(s, h))
            tasks.append(conv_a_task(s, h * HEAD_DIM))
    return tasks


def _scan_stages(ops, on_ref, yb_ref, s_scr):
    C = CHUNK
    ns = yb_ref.shape[0]
    row = lax.broadcasted_iota(jnp.int32, (C, C), 0)
    col = lax.broadcasted_iota(jnp.int32, (C, C), 1)
    strict = (row > col)[None]
    eye = (row == col).astype(F32)[None]

    k16 = ops.k16[...]
    dec_causal = ops.dec[...]
    lmat = _bdot_nt(ops.kb16[...], k16) * jnp.where(strict, dec_causal, 0.0)
    attn = (_bdot_nt(ops.q16[...], k16) * dec_causal).astype(BF16)
    nh = k16.shape[0]
    token_of = lambda a: a[nh - 1, C - 1:C, :HEAD_DIM]
    yield None
    for tmat, w_level in _unit_lower_inverse_levels(lmat, eye):
        yield token_of(w_level)
    uw = _bdot(tmat.astype(BF16), ops.rhs16[...])
    yield token_of(uw)
    s_old = s_scr[...]
    lhs = jnp.concatenate([uw[:, :, HEAD_DIM:].astype(BF16), ops.qeg16[...]], axis=1)
    ws = _bdot(lhs, s_old.astype(BF16))
    yield token_of(ws)
    u = uw[:, :, :HEAD_DIM] - ws[:, :C]
    u16 = u.astype(BF16)
    o = ws[:, C:] + _bdot(attn, u16)
    s_scr[...] = s_old * ops.sdec[...] + _bdot_tn(ops.kt16[...], u16)
    yield token_of(o)
    o = o * lax.rsqrt(jnp.mean(o * o, axis=-1, keepdims=True) + EPS) * on_ref[...]
    for s in range(ns):
        for h in range(HEADS):
            sl = slice(h * HEAD_DIM, (h + 1) * HEAD_DIM)
            yb_ref[s, :, sl] = (o[s * HEADS + h] * ops.gate[s, :, sl]).astype(yb_ref.dtype)


def _interleave(stages, tasks, per_gap):
    todo = list(tasks)
    token = None
    for n, token in zip(list(per_gap) + [0] * 64, stages):
        for task in todo[:n]:
            task(token)
        todo = todo[n:]
    for task in todo:
        task(token)


def _mixer_prompt_kernel(z_ref, zt_ref, wa_ref, wq_ref, alog_ref, dtb_ref, on_ref,
                         ya_ref, yb_ref, ca_ref, cq_ref, ss_ref, buf_a, buf_q, s_scr, *operand_refs):
    t = pl.program_id(1)
    ns = z_ref.shape[0]
    n_fields = len(_ChunkOperands._fields)
    sets = (_ChunkOperands(*operand_refs[:n_fields]), _ChunkOperands(*operand_refs[n_fields:]))

    @pl.when(t == 0)
    def _():
        buf_a[:, 0:HALO, :] = jnp.zeros((ns, HALO, D_CONV), F32)
        buf_q[:, 0:HALO, :] = jnp.zeros((ns, HALO, 3 * D_GDN), F32)
        s_scr[...] = jnp.zeros(s_scr.shape, F32)
        for ref in sets[1]:
            ref[...] = jnp.zeros(ref.shape, ref.dtype)

    for parity in range(2):
        @pl.when(t % 2 == parity)
        def _(parity=parity):
            _interleave(
                _scan_stages(sets[1 - parity], on_ref, yb_ref, s_scr),
                _prep_tasks(z_ref, zt_ref, wa_ref, wq_ref, alog_ref, dtb_ref, ya_ref, ca_ref,
                            cq_ref, buf_a, buf_q, sets[parity]),
                PREP_TASKS_PER_GAP)

    @pl.when(t == pl.num_programs(1) - 1)
    def _():
        ss_ref[...] = s_scr[...].reshape(ss_ref.shape)


def _skip_refs(n, kernel, *refs):
    kernel(*refs[n:])


def _mixer_prompt_call(layer, z, zt, conv_a_w, conv_qkv_w, alog_rows, dtb_rows, onorm3, batch, seq,
                       stacks):
    nc = seq // CHUNK
    ns = PROMPT_STREAMS
    lsel = lambda g, t: (layer, 0, 0)
    cur = lambda g, t: (g, jnp.minimum(t, nc - 1), 0)
    prev = lambda g, t: (g, jnp.maximum(t - 1, 0), 0)
    scratch = _operand_scratch(ns)
    chained = () if stacks is None else tuple(stacks)
    n_chain = len(chained)
    return pl.pallas_call(
        functools.partial(_skip_refs, n_chain, _mixer_prompt_kernel),
        grid=(batch // ns, nc + 1),
        input_output_aliases={i: 2 + i for i in range(n_chain)},
        in_specs=[pl.BlockSpec(memory_space=pl.ANY)] * n_chain + [
            pl.BlockSpec((ns, CHUNK, D_MAIN), cur),
            pl.BlockSpec((ns, CHUNK, LANES), cur),
            pl.BlockSpec((None, 3, D_CONV), lsel),
            pl.BlockSpec((None, 4, 3 * D_GDN), lsel),
            pl.BlockSpec((None, 1, LANES), lsel),
            pl.BlockSpec((None, 1, LANES), lsel),
            pl.BlockSpec((None, 1, HEAD_DIM), lsel),
        ],
        out_specs=[
            pl.BlockSpec((ns, CHUNK, D_CONV), lambda g, t: (g, t, 0)),
            pl.BlockSpec((ns, CHUNK, D_GDN), prev),
            pl.BlockSpec((None, ns, 2, D_CONV), lambda g, t: (layer, g, 0, 0)),
            pl.BlockSpec((None, ns, 3, 3 * D_GDN), lambda g, t: (layer, g, 0, 0)),
            pl.BlockSpec((None, ns, HEADS, HEAD_DIM, HEAD_DIM), lambda g, t: (layer, g, 0, 0, 0)),
        ],
        out_shape=[
            jax.ShapeDtypeStruct((batch, seq + CHUNK, D_CONV), BF16),
            jax.ShapeDtypeStruct((batch, seq, D_GDN), BF16),
            jax.ShapeDtypeStruct((DEPTH, batch, 2, D_CONV), F32),
            jax.ShapeDtypeStruct((DEPTH, batch, 3, 3 * D_GDN), F32),
            jax.ShapeDtypeStruct((DEPTH, batch, HEADS, HEAD_DIM, HEAD_DIM), F32),
        ],
        scratch_shapes=[
            pltpu.VMEM((ns, HALO + CHUNK, D_CONV), F32),
            pltpu.VMEM((ns, HALO + CHUNK, 3 * D_GDN), F32),
            pltpu.VMEM((ns * HEADS, HEAD_DIM, HEAD_DIM), F32),
            *scratch, *scratch,
        ],
        compiler_params=pltpu.CompilerParams(
            dimension_semantics=("arbitrary", "arbitrary"), vmem_limit_bytes=VMEM_LIMIT),
        name="mixer_prompt",
    )(*chained, z.reshape(batch, seq, D_MAIN), zt.reshape(batch, seq, LANES), conv_a_w, conv_qkv_w,
      alog_rows, dtb_rows, onorm3)


def _mixer_sample_kernel(z_ref, zt_ref, sa_ref, sq_ref, s_ref, wa_ref, wq_ref, alog_ref, dtb_ref,
                         on_ref, y_ref, na_ref, nq_ref, ns_ref):
    nb = z_ref.shape[0]

    ch = z_ref[:, OFF_CA:OFF_CA + D_CONV] * z_ref[:, OFF_HA:OFF_HA + D_CONV]
    prev0 = sa_ref[:, 0:D_CONV]
    prev1 = sa_ref[:, D_CONV:2 * D_CONV]
    conv = prev0 * wa_ref[0:1, :] + prev1 * wa_ref[1:2, :] + ch * wa_ref[2:3, :]
    y_ref[:, 0:D_CONV] = (z_ref[:, OFF_BA:OFF_BA + D_CONV] * conv
                          * z_ref[:, OFF_GA:OFF_GA + D_CONV])
    na_ref[:, 0:D_CONV] = prev1
    na_ref[:, D_CONV:2 * D_CONV] = ch

    w3 = 3 * D_GDN
    nq_ref[:, 0:w3] = sq_ref[:, w3:2 * w3]
    nq_ref[:, w3:2 * w3] = sq_ref[:, 2 * w3:3 * w3]
    nq_ref[:, 2 * w3:3 * w3] = z_ref[:, OFF_QKV:OFF_QKV + w3]

    beta_all, g_all = _gate_terms(zt_ref[...], alog_ref[...], dtb_ref[...])
    eg_all = jnp.exp(g_all)
    rowid = lax.broadcasted_iota(jnp.int32, (nb, HEAD_DIM), 0)

    def conv_qkv(off):
        sl = slice(off, off + HEAD_DIM)
        acc = z_ref[:, OFF_QKV + off:OFF_QKV + off + HEAD_DIM] * wq_ref[3:4, sl]
        for j in range(3):
            acc = acc + sq_ref[:, j * w3 + off:j * w3 + off + HEAD_DIM] * wq_ref[j:j + 1, sl]
        return _silu_of_half(acc)

    qn, kn, vv, betas, egs = [], [], [], [], []
    for h in range(HEADS):
        q = conv_qkv(h * HEAD_DIM)
        k = conv_qkv(D_GDN + h * HEAD_DIM)
        v = conv_qkv(2 * D_GDN + h * HEAD_DIM)
        qn.append(q * (lax.rsqrt(jnp.sum(q * q, axis=-1, keepdims=True) + EPS) * (HEAD_DIM ** -0.5)))
        kn.append(k * lax.rsqrt(jnp.sum(k * k, axis=-1, keepdims=True) + EPS))
        vv.append(v)
        betas.append(beta_all[:, h:h + 1])
        egs.append(eg_all[:, HEADS + h:HEADS + h + 1])

    pairs = [(j, h) for j in range(nb) for h in range(HEADS)]
    s_old = s_ref[...].reshape(nb * HEADS, HEAD_DIM, HEAD_DIM)
    kq = jnp.stack([jnp.concatenate([kn[h], qn[h]], axis=0) for _, h in pairs], axis=0)
    r = _bdot(kq, s_old)
    us = []
    for h in range(HEADS):
        ks = jnp.zeros((nb, HEAD_DIM), F32)
        qs = jnp.zeros((nb, HEAD_DIM), F32)
        for j in range(nb):
            rp = r[j * HEADS + h]
            ks = jnp.where(rowid == j, rp[:nb], ks)
            qs = jnp.where(rowid == j, rp[nb:], qs)
        u = betas[h] * (vv[h] - egs[h] * ks)
        o = egs[h] * qs + jnp.sum(qn[h] * kn[h], axis=-1, keepdims=True) * u
        o = o * lax.rsqrt(jnp.mean(o * o, axis=-1, keepdims=True) + EPS) * on_ref[...]
        gb = z_ref[:, OFF_GB + h * HEAD_DIM:OFF_GB + (h + 1) * HEAD_DIM]
        y_ref[:, D_CONV + h * HEAD_DIM:D_CONV + (h + 1) * HEAD_DIM] = o * gb
        us.append(u)
    k_own = jnp.stack([jnp.where(rowid == j, kn[h], 0.0) for j, h in pairs], axis=0)
    u_all = jnp.stack([us[h] for _, h in pairs], axis=0)
    decay = jnp.stack([jnp.broadcast_to(egs[h][j:j + 1, :], (1, HEAD_DIM))
                       for j, h in pairs], axis=0)
    s_new = s_old * decay + _bdot_tn(k_own, u_all)
    ns_ref[...] = s_new.reshape(ns_ref.shape)


def _mixer_sample_call(layer, z, zt, sa_all, sq_all, state_ssm, conv_a_w, conv_qkv_w, alog_rows,
                       dtb_rows, onorm3, stacks):
    nseq = z.shape[0]
    nb = 8
    lsel = lambda i: (layer, 0, 0)
    chained = () if stacks is None else tuple(stacks)
    n_chain = len(chained)
    return pl.pallas_call(
        functools.partial(_skip_refs, n_chain, _mixer_sample_kernel),
        grid=(nseq // nb,),
        input_output_aliases={i: 1 + i for i in range(n_chain)},
        in_specs=[pl.BlockSpec(memory_space=pl.ANY)] * n_chain + [
            pl.BlockSpec((nb, D_MAIN), lambda i: (i, 0)),
            pl.BlockSpec((nb, LANES), lambda i: (i, 0)),
            pl.BlockSpec((None, nb, 2 * D_CONV), lambda i: (layer, i, 0)),
            pl.BlockSpec((None, nb, 9 * D_GDN), lambda i: (layer, i, 0)),
            pl.BlockSpec((None, nb, HEADS, HEAD_DIM, HEAD_DIM), lambda i: (layer, i, 0, 0, 0)),
            pl.BlockSpec((None, 3, D_CONV), lsel),
            pl.BlockSpec((None, 4, 3 * D_GDN), lsel),
            pl.BlockSpec((None, 1, LANES), lsel),
            pl.BlockSpec((None, 1, LANES), lsel),
            pl.BlockSpec((None, 1, HEAD_DIM), lsel),
        ],
        out_specs=[
            pl.BlockSpec((nb, D_MODEL), lambda i: (i, 0)),
            pl.BlockSpec((None, nb, 2 * D_CONV), lambda i: (layer, i, 0)),
            pl.BlockSpec((None, nb, 9 * D_GDN), lambda i: (layer, i, 0)),
            pl.BlockSpec((None, nb, HEADS, HEAD_DIM, HEAD_DIM), lambda i: (layer, i, 0, 0, 0)),
        ],
        out_shape=[
            jax.ShapeDtypeStruct((nseq, D_MODEL), F32),
            jax.ShapeDtypeStruct((DEPTH, nseq, 2 * D_CONV), F32),
            jax.ShapeDtypeStruct((DEPTH, nseq, 9 * D_GDN), F32),
            jax.ShapeDtypeStruct((DEPTH, nseq, HEADS, HEAD_DIM, HEAD_DIM), F32),
        ],
        compiler_params=pltpu.CompilerParams(
            dimension_semantics=("arbitrary",), vmem_limit_bytes=VMEM_LIMIT),
        name="mixer_sample",
    )(*chained, z, zt, sa_all, sq_all, state_ssm, conv_a_w, conv_qkv_w, alog_rows, dtb_rows, onorm3)


def kernel(x_prompt, x_sample, state_conv_a, state_conv_qkv, state_ssm, c_prompt, c_sample, norm_g, w_ada, b_ada, w_in, conv_a_w, conv_qkv_w, a_log, dt_bias, o_norm_g, w_out, final_norm_g):
    batch, seq, _ = x_prompt.shape
    nseq = x_sample.shape[0]
    assert x_sample.shape[1] == 1 and seq % CHUNK == 0

    d_in_proj = w_in.shape[2]
    w_main = _cast_call(jnp.swapaxes(w_in, 1, 2), d_in_proj // 9)
    w_tail = w_main
    w_out16 = w_out.astype(BF16)
    norm_g3 = norm_g.reshape(DEPTH, 1, D_MODEL)
    onorm3 = o_norm_g.reshape(DEPTH, 1, HEAD_DIM)
    final_g = final_norm_g.reshape(1, D_MODEL)
    conv_qkv_half = 0.5 * conv_qkv_w
    pad_heads = ((0, 0), (HEADS, LANES - 2 * HEADS))
    alog_rows = jnp.pad(a_log, pad_heads).reshape(DEPTH, 1, LANES)
    dtb_rows = jnp.pad(dt_bias, pad_heads).reshape(DEPTH, 1, LANES)

    n_cond = nseq + batch
    n_cond_pad = -(-n_cond // 8) * 8
    c_all = jnp.concatenate(
        [c_sample, c_prompt, jnp.zeros((n_cond_pad - n_cond, D_MODEL), F32)], axis=0)
    mod = _mod_call(c_all, w_ada, b_ada)

    xp = x_prompt.reshape(batch * seq, D_MODEL)
    xs = x_sample.reshape(nseq, D_MODEL)
    tm_in = 1024
    tm_out = 512
    sa_all = state_conv_a.reshape(DEPTH, nseq, 2 * D_CONV)
    sq_all = state_conv_qkv.reshape(DEPTH, nseq, 9 * D_GDN)
    prompt_states = sample_states = None

    for layer in range(DEPTH):
        def pmod(sec, per_batch, layer=layer):
            return pl.BlockSpec((None, 8, D_MODEL),
                                lambda i, *_: (layer, (nseq + i // per_batch) // 8, sec))

        def prow(per_batch):
            return lambda i: (nseq + i // per_batch) % 8
        z, zt = _inproj_call(layer, xp, norm_g3, mod, (pmod(1, seq // tm_in), pmod(0, seq // tm_in)),
                             prow(seq // tm_in), w_main, w_tail, tm_in)
        ya, yb, *prompt_states = _mixer_prompt_call(layer, z, zt, conv_a_w, conv_qkv_half, alog_rows,
                                                    dtb_rows, onorm3, batch, seq, prompt_states)
        per_seq = seq // tm_out
        y_specs = [pl.BlockSpec((None, tm_out, D_CONV), lambda i: (i // per_seq, i % per_seq, 0)),
                   pl.BlockSpec((None, tm_out, D_GDN), lambda i: (i // per_seq, i % per_seq, 0))]
        xp = _outproj_call(layer, ya, yb, y_specs, xp, mod, pmod(2, per_seq), prow(per_seq), w_out16,
                           final_g, tm_out)

        def smod(sec, layer=layer):
            return pl.BlockSpec((None, nseq, D_MODEL), lambda i, *_: (layer, 0, sec))
        z, zt = _inproj_call(layer, xs, norm_g3, mod, (smod(1), smod(0)), None, w_main, w_tail, nseq)
        y, *sample_states = _mixer_sample_call(layer, z, zt, sa_all, sq_all, state_ssm, conv_a_w,
                                               conv_qkv_half, alog_rows, dtb_rows, onorm3,
                                               sample_states)
        y_specs = [pl.BlockSpec((nseq, D_CONV), lambda i: (0, 0)),
                   pl.BlockSpec((nseq, D_GDN), lambda i: (0, 1))]
        xs = _outproj_call(layer, y, y, y_specs, xs, mod, smod(2), None, w_out16, final_g, nseq)

    ca_p, cq_p, ss_p = prompt_states
    na, nq, ss_s = sample_states
    return (xp.reshape(batch, seq, D_MODEL), xs.reshape(nseq, 1, D_MODEL), ca_p, cq_p, ss_p,
            na.reshape(DEPTH, nseq, 2, D_CONV), nq.reshape(DEPTH, nseq, 3, 3 * D_GDN), ss_s)
```

```python
import functools
from typing import NamedTuple

import jax
import jax.numpy as jnp
from jax import lax
from jax.experimental import pallas as pl
from jax.experimental.pallas import tpu as pltpu

F32 = jnp.float32
BF16 = jnp.bfloat16

D_MODEL = 2048
DEPTH = 4
D_CONV = 1024
D_GDN = 1024
HEADS = 8
HEAD_DIM = 128
CHUNK = 64
EPS = 1e-6
D_MAIN = 8192
OFF_BA, OFF_CA, OFF_HA, OFF_GA, OFF_QKV, OFF_GB = 0, 1024, 2048, 3072, 4096, 7168
LANES = 128
HALO = 8
VMEM_LIMIT = 56 * 1024 * 1024
VMEM_LIMIT_BIG_TILES = 60 * 1024 * 1024
PROMPT_STREAMS = 2
SAMPLE_SEQS_PER_STEP = 16
PREP_TASKS_PER_GAP = (8, 5, 5, 5, 4, 3, 2, 2, 0, 0)


def _silu_of_half(hx):
    return hx + hx * jnp.tanh(hx)


def _silu(x):
    return _silu_of_half(0.5 * x)


def _softplus(x):
    return jnp.maximum(x, 0.0) + jnp.log1p(jnp.exp(-jnp.abs(x)))


def _dot(a, b):
    return jnp.dot(a, b, preferred_element_type=F32)


def _dot_nt(a, b):
    return lax.dot_general(a, b, (((1,), (1,)), ((), ())), preferred_element_type=F32)


def _dot_hi(a, b):
    return jnp.dot(a, b, preferred_element_type=F32, precision=lax.Precision.HIGHEST)


def _mod_kernel(c_ref, w_ref, b_ref, o_ref):
    s = _silu(c_ref[...]).astype(BF16)
    o_ref[...] = _dot(s, w_ref[...].astype(BF16)) + b_ref[...]


def _mod_call(c_all, w_ada, b_ada):
    rows = c_all.shape[0]
    tn = 1024
    return pl.pallas_call(
        _mod_kernel,
        grid=(DEPTH, 3 * D_MODEL // tn),
        in_specs=[
            pl.BlockSpec((rows, D_MODEL), lambda l, j: (0, 0)),
            pl.BlockSpec((None, D_MODEL, tn), lambda l, j: (l, 0, j)),
            pl.BlockSpec((None, 1, tn), lambda l, j: (l, 0, j)),
        ],
        out_specs=pl.BlockSpec((None, rows, tn), lambda l, j: (l, 0, j)),
        out_shape=jax.ShapeDtypeStruct((DEPTH, rows, 3 * D_MODEL), F32),
        compiler_params=pltpu.CompilerParams(
            dimension_semantics=("arbitrary", "arbitrary"), vmem_limit_bytes=VMEM_LIMIT),
        name="adaln_mod",
    )(c_all, w_ada, b_ada.reshape(DEPTH, 1, 3 * D_MODEL))


def _cast_transpose_kernel(w_ref, o_ref):
    o_ref[...] = w_ref[...].T.astype(BF16)


def _cast_transpose_call(wt, rows, tr):
    depth, _, cols = wt.shape
    assert rows % tr == 0
    return pl.pallas_call(
        _cast_transpose_kernel,
        grid=(depth, rows // tr),
        in_specs=[pl.BlockSpec((None, tr, cols), lambda l, i: (l, i, 0))],
        out_specs=pl.BlockSpec((None, cols, tr), lambda l, i: (l, 0, i)),
        out_shape=jax.ShapeDtypeStruct((depth, cols, rows), BF16),
        compiler_params=pltpu.CompilerParams(
            dimension_semantics=("arbitrary", "arbitrary"), vmem_limit_bytes=VMEM_LIMIT),
        name="cast_bf16",
    )(wt)


def _cond_rows(ref, cond_row):
    if cond_row is None:
        return ref[...]
    return ref[pl.ds(cond_row(pl.program_id(0)), 1), :]


def _inproj_kernel(x_ref, g_ref, sc_ref, sh_ref, w_ref, wt_ref, z_ref, zt_ref, h_scr, *, cond_row):
    @pl.when(pl.program_id(1) == 0)
    def _():
        x = x_ref[...]
        y = x * lax.rsqrt(jnp.mean(x * x, axis=-1, keepdims=True) + EPS) * g_ref[...]
        h = (y * (1.0 + _cond_rows(sc_ref, cond_row)) + _cond_rows(sh_ref, cond_row)).astype(BF16)
        h_scr[...] = h
        zt = _dot_nt(h, wt_ref[...])
        zt_ref[...] = jnp.concatenate(
            [zt, jnp.zeros((zt.shape[0], LANES - zt.shape[1]), F32)], axis=1)

    z = _dot(h_scr[...], w_ref[...])
    sections = z_ref.shape[1] // D_CONV
    for s in range(sections):
        sl = slice(s * D_CONV, (s + 1) * D_CONV)
        is_gate = (pl.program_id(1) * sections + s) % 4 == 3
        z_ref[:, sl] = jnp.where(is_gate, _silu(z[:, sl]), z[:, sl])


def _inproj_call(layer, x, norm_g3, mod_arr, mod_specs, cond_row, w_main, w_tail, tm):
    rows = x.shape[0]
    tn = 2 * D_CONV if tm >= 1024 else D_CONV
    sc_spec, sh_spec = mod_specs
    return pl.pallas_call(
        functools.partial(_inproj_kernel, cond_row=cond_row),
        grid=(rows // tm, D_MAIN // tn),
        in_specs=[
            pl.BlockSpec((tm, D_MODEL), lambda i, j: (i, 0)),
            pl.BlockSpec((None, 1, D_MODEL), lambda i, j: (layer, 0, 0)),
            sc_spec,
            sh_spec,
            pl.BlockSpec((None, D_MODEL, tn), lambda i, j: (layer, 0, j)),
            pl.BlockSpec((None, 2 * HEADS, D_MODEL), lambda i, j: (layer, 0, 0)),
        ],
        out_specs=[
            pl.BlockSpec((tm, tn), lambda i, j: (i, j)),
            pl.BlockSpec((tm, LANES), lambda i, j: (i, 0)),
        ],
        out_shape=[
            jax.ShapeDtypeStruct((rows, D_MAIN), F32),
            jax.ShapeDtypeStruct((rows, LANES), F32),
        ],
        scratch_shapes=[pltpu.VMEM((tm, D_MODEL), BF16)],
        compiler_params=pltpu.CompilerParams(
            dimension_semantics=("arbitrary", "arbitrary"), vmem_limit_bytes=VMEM_LIMIT_BIG_TILES),
        name="in_proj",
    )(x, norm_g3, mod_arr, mod_arr, w_main, w_tail)


def _outproj_kernel(ya_ref, yb_ref, x_ref, gate_ref, wa_ref, wb_ref, fg_ref, o_ref, *, final,
                    cond_row):
    acc = (_dot(ya_ref[...].astype(BF16), wa_ref[...])
           + _dot(yb_ref[...].astype(BF16), wb_ref[...]))
    xn = x_ref[...] + _cond_rows(gate_ref, cond_row) * acc
    if final:
        xn = xn * lax.rsqrt(jnp.mean(xn * xn, axis=-1, keepdims=True) + EPS) * fg_ref[...]
    o_ref[...] = xn


def _outproj_call(layer, ya, yb, y_specs, x, mod_arr, gate_spec, cond_row, w_out, final_g, tm):
    rows = x.shape[0]
    half = D_MODEL // 2
    return pl.pallas_call(
        functools.partial(_outproj_kernel, final=(layer == DEPTH - 1), cond_row=cond_row),
        grid=(rows // tm,),
        in_specs=[
            *y_specs,
            pl.BlockSpec((tm, D_MODEL), lambda i: (i, 0)),
            gate_spec,
            pl.BlockSpec((None, half, D_MODEL), lambda i: (layer, 0, 0), pipeline_mode=pl.Buffered(1)),
            pl.BlockSpec((None, half, D_MODEL), lambda i: (layer, 1, 0), pipeline_mode=pl.Buffered(1)),
            pl.BlockSpec((1, D_MODEL), lambda i: (0, 0)),
        ],
        out_specs=pl.BlockSpec((tm, D_MODEL), lambda i: (i, 0)),
        out_shape=jax.ShapeDtypeStruct((rows, D_MODEL), F32),
        compiler_params=pltpu.CompilerParams(
            dimension_semantics=("arbitrary",), vmem_limit_bytes=VMEM_LIMIT_BIG_TILES),
        name="out_proj",
    )(ya, yb, x, mod_arr, w_out, w_out, final_g)


def _gate_terms(tail, alog_row, dtb_row):
    beta = jax.nn.sigmoid(tail)
    g = -jnp.exp(alog_row) * _softplus(tail + dtb_row)
    return beta, g


def _bdot(a, b):
    return lax.dot_general(a, b, (((2,), (1,)), ((0,), (0,))), preferred_element_type=F32)


def _bdot_nt(a, b):
    return lax.dot_general(a, b, (((2,), (2,)), ((0,), (0,))), preferred_element_type=F32)


def _bdot_tn(a, b):
    return lax.dot_general(a, b, (((1,), (1,)), ((0,), (0,))), preferred_element_type=F32)


def _split_bf16(a):
    hi = a.astype(BF16)
    lo = (a - hi.astype(F32)).astype(BF16)
    return hi, lo


def _unit_lower_inverse_levels(lmat, eye):
    heads, c, _ = lmat.shape
    eye_b = jnp.broadcast_to(eye, lmat.shape)
    pass_x = jnp.concatenate([eye_b, jnp.zeros_like(eye_b)], axis=2).astype(BF16)
    zeros_top = jnp.zeros((heads, c, 2 * c), BF16)
    w = jnp.concatenate([eye_b, -lmat], axis=2)
    for _ in range(6):
        w_hi, w_lo = _split_bf16(w)
        lhs = jnp.concatenate([w_hi, w_lo, w_hi], axis=2)
        rhs = jnp.concatenate([pass_x, w_hi, pass_x, w_hi, zeros_top, w_lo], axis=1)
        w = _bdot(lhs, rhs)
        yield w[:, :, :c], w


class _ChunkOperands(NamedTuple):
    k16: object
    kb16: object
    q16: object
    rhs16: object
    qeg16: object
    kt16: object
    dec: object
    sdec: object
    gate: object


def _operand_scratch(ns):
    nh = ns * HEADS
    return _ChunkOperands(
        k16=pltpu.VMEM((nh, CHUNK, HEAD_DIM), BF16),
        kb16=pltpu.VMEM((nh, CHUNK, HEAD_DIM), BF16),
        q16=pltpu.VMEM((nh, CHUNK, HEAD_DIM), BF16),
        rhs16=pltpu.VMEM((nh, CHUNK, 2 * HEAD_DIM), BF16),
        qeg16=pltpu.VMEM((nh, CHUNK, HEAD_DIM), BF16),
        kt16=pltpu.VMEM((nh, CHUNK, HEAD_DIM), BF16),
        dec=pltpu.VMEM((nh, CHUNK, CHUNK), F32),
        sdec=pltpu.VMEM((nh, 1, HEAD_DIM), F32),
        gate=pltpu.VMEM((ns, CHUNK, D_GDN), F32),
    )


def _prep_tasks(z_ref, zt_ref, wa_ref, wq_ref, alog_ref, dtb_ref, ya_ref, ca_ref, cq_ref,
                buf_a, buf_q, out):
    C = CHUNK
    ns = z_ref.shape[0]
    row = lax.broadcasted_iota(jnp.int32, (C, C), 0)
    col = lax.broadcasted_iota(jnp.int32, (C, C), 1)
    causal = row >= col
    tri = causal.astype(F32)
    sub = lax.broadcasted_iota(jnp.int32, (C // 8, 8, HEAD_DIM), 1)
    never = pl.program_id(1) < 0

    def pinned(row_vec, token):
        return row_vec if token is None else jnp.where(never, token, row_vec)

    def shifted(x3, d):
        r = pltpu.roll(x3, d, axis=1)
        return jnp.where(sub < d, r[:-1], r[1:]).reshape(C, x3.shape[-1])

    def haloed(buf, s, sl):
        x = buf[s, :, sl]
        return x, x.reshape((HALO + C) // 8, 8, HEAD_DIM)

    gates = {}

    def gate_task(s):
        def run(token):
            del token
            beta_s, g_s = _gate_terms(zt_ref[s], alog_ref[...], dtb_ref[...])
            gc_s = _dot_hi(tri, g_s)
            gates[s] = (beta_s, gc_s, gc_s.T)
            buf_a[s, HALO:HALO + C, :] = (z_ref[s, :, OFF_CA:OFF_CA + D_CONV]
                                          * z_ref[s, :, OFF_HA:OFF_HA + D_CONV])
            ca_ref[s] = buf_a[s, HALO + C - 2:HALO + C, :]
            buf_q[s, HALO:HALO + C, :] = z_ref[s, :, OFF_QKV:OFF_QKV + 3 * D_GDN]
            cq_ref[s] = buf_q[s, HALO + C - 3:HALO + C, :]
        return run

    def conv_a_task(s, off):
        def run(token):
            sl = slice(off, off + HEAD_DIM)
            xa, xa3 = haloed(buf_a, s, sl)
            conv = (shifted(xa3, 2) * pinned(wa_ref[0:1, sl], token)
                    + shifted(xa3, 1) * pinned(wa_ref[1:2, sl], token)
                    + xa[HALO:] * pinned(wa_ref[2:3, sl], token))
            ya = (z_ref[s, :, OFF_BA + off:OFF_BA + off + HEAD_DIM] * conv
                  * z_ref[s, :, OFF_GA + off:OFF_GA + off + HEAD_DIM])
            ya_ref[s, :, sl] = ya.astype(ya_ref.dtype)
            buf_a[s, 0:HALO, sl] = xa[C:]
        return run

    def conv_qkv(s, off, token):
        sl = slice(off, off + HEAD_DIM)
        x, x3 = haloed(buf_q, s, sl)
        acc = x[HALO:] * pinned(wq_ref[3:4, sl], token)
        for d in range(1, 4):
            acc = acc + shifted(x3, d) * pinned(wq_ref[3 - d:4 - d, sl], token)
        buf_q[s, 0:HALO, sl] = x[C:]
        return _silu_of_half(acc)

    def head_task(s, h):
        def run(token):
            n = s * HEADS + h
            beta_s, gc_s, gc_t = gates[s]
            sl = slice(h * HEAD_DIM, (h + 1) * HEAD_DIM)
            q = conv_qkv(s, h * HEAD_DIM, token)
            k = conv_qkv(s, D_GDN + h * HEAD_DIM, token)
            v = conv_qkv(s, 2 * D_GDN + h * HEAD_DIM, token)
            q = q * (lax.rsqrt(jnp.sum(q * q, axis=-1, keepdims=True) + EPS) * (HEAD_DIM ** -0.5))
            k = k * lax.rsqrt(jnp.sum(k * k, axis=-1, keepdims=True) + EPS)
            beta = jnp.broadcast_to(beta_s[:, h:h + 1], (C, HEAD_DIM))
            gcol = jnp.broadcast_to(gc_s[:, HEADS + h:HEADS + h + 1], (C, HEAD_DIM))
            grow = gc_t[HEADS + h:HEADS + h + 1, :]
            glast = gcol[C - 1:C, :]
            eg = jnp.exp(gcol)
            kb = k * beta
            out.k16[n] = k.astype(BF16)
            out.kb16[n] = kb.astype(BF16)
            out.q16[n] = q.astype(BF16)
            out.rhs16[n] = jnp.concatenate([v * beta, kb * eg], axis=1).astype(BF16)
            out.qeg16[n] = (q * eg).astype(BF16)
            out.kt16[n] = (k * jnp.exp(glast - gcol)).astype(BF16)
            out.dec[n] = jnp.exp(jnp.where(causal, gcol[:, :C] - grow, -jnp.inf))
            out.sdec[n] = jnp.exp(glast)
            out.gate[s, :, sl] = z_ref[s, :, OFF_GB + h * HEAD_DIM:OFF_GB + (h + 1) * HEAD_DIM]
        return run

    tasks = [gate_task(s) for s in range(ns)]
    for s in range(ns):
        for h in range(HEADS):
            tasks.append(head_task(s, h))
            tasks.append(conv_a_task(s, h * HEAD_DIM))
    return tasks


def _scan_stages(ops, on_ref, yb_ref, s_scr):
    C = CHUNK
    ns = yb_ref.shape[0]
    row = lax.broadcasted_iota(jnp.int32, (C, C), 0)
    col = lax.broadcasted_iota(jnp.int32, (C, C), 1)
    strict = (row > col)[None]
    eye = (row == col).astype(F32)[None]

    k16 = ops.k16[...]
    dec_causal = ops.dec[...]
    lmat = _bdot_nt(ops.kb16[...], k16) * jnp.where(strict, dec_causal, 0.0)
    attn = (_bdot_nt(ops.q16[...], k16) * dec_causal).astype(BF16)
    nh = k16.shape[0]
    token_of = lambda a: a[nh - 1, C - 1:C, :HEAD_DIM]
    yield None
    for tmat, w_level in _unit_lower_inverse_levels(lmat, eye):
        yield token_of(w_level)
    uw = _bdot(tmat.astype(BF16), ops.rhs16[...])
    yield token_of(uw)
    s_old = s_scr[...]
    lhs = jnp.concatenate([uw[:, :, HEAD_DIM:].astype(BF16), ops.qeg16[...]], axis=1)
    ws = _bdot(lhs, s_old.astype(BF16))
    yield token_of(ws)
    u = uw[:, :, :HEAD_DIM] - ws[:, :C]
    u16 = u.astype(BF16)
    o = ws[:, C:] + _bdot(attn, u16)
    s_scr[...] = s_old * ops.sdec[...] + _bdot_tn(ops.kt16[...], u16)
    yield token_of(o)
    o = o * lax.rsqrt(jnp.mean(o * o, axis=-1, keepdims=True) + EPS) * on_ref[...]
    for s in range(ns):
        for h in range(HEADS):
            sl = slice(h * HEAD_DIM, (h + 1) * HEAD_DIM)
            yb_ref[s, :, sl] = (o[s * HEADS + h] * ops.gate[s, :, sl]).astype(yb_ref.dtype)


def _interleave(stages, tasks, per_gap):
    todo = list(tasks)
    token = None
    for n, token in zip(list(per_gap) + [0] * 64, stages):
        for task in todo[:n]:
            task(token)
        todo = todo[n:]
    for task in todo:
        task(token)


def _mixer_prompt_kernel(z_ref, zt_ref, wa_ref, wq_ref, alog_ref, dtb_ref, on_ref,
                         ya_ref, yb_ref, ca_ref, cq_ref, ss_ref, buf_a, buf_q, s_scr, *operand_refs):
    t = pl.program_id(1)
    ns = z_ref.shape[0]
    n_fields = len(_ChunkOperands._fields)
    sets = (_ChunkOperands(*operand_refs[:n_fields]), _ChunkOperands(*operand_refs[n_fields:]))

    @pl.when(t == 0)
    def _():
        buf_a[:, 0:HALO, :] = jnp.zeros((ns, HALO, D_CONV), F32)
        buf_q[:, 0:HALO, :] = jnp.zeros((ns, HALO, 3 * D_GDN), F32)
        s_scr[...] = jnp.zeros(s_scr.shape, F32)
        for ref in sets[1]:
            ref[...] = jnp.zeros(ref.shape, ref.dtype)

    for parity in range(2):
        @pl.when(t % 2 == parity)
        def _(parity=parity):
            _interleave(
                _scan_stages(sets[1 - parity], on_ref, yb_ref, s_scr),
                _prep_tasks(z_ref, zt_ref, wa_ref, wq_ref, alog_ref, dtb_ref, ya_ref, ca_ref,
                            cq_ref, buf_a, buf_q, sets[parity]),
                PREP_TASKS_PER_GAP)

    @pl.when(t == pl.num_programs(1) - 1)
    def _():
        ss_ref[...] = s_scr[...].reshape(ss_ref.shape)


def _skip_refs(n, kernel, *refs):
    kernel(*refs[n:])


def _mixer_prompt_call(layer, z, zt, conv_a_w, conv_qkv_w, alog_rows, dtb_rows, onorm3, batch, seq,
                       stacks):
    nc = seq // CHUNK
    ns = PROMPT_STREAMS
    lsel = lambda g, t: (layer, 0, 0)
    cur = lambda g, t: (g, jnp.minimum(t, nc - 1), 0)
    prev = lambda g, t: (g, jnp.maximum(t - 1, 0), 0)
    scratch = _operand_scratch(ns)
    chained = () if stacks is None else tuple(stacks)
    n_chain = len(chained)
    return pl.pallas_call(
        functools.partial(_skip_refs, n_chain, _mixer_prompt_kernel),
        grid=(batch // ns, nc + 1),
        input_output_aliases={i: 2 + i for i in range(n_chain)},
        in_specs=[pl.BlockSpec(memory_space=pl.ANY)] * n_chain + [
            pl.BlockSpec((ns, CHUNK, D_MAIN), cur),
            pl.BlockSpec((ns, CHUNK, LANES), cur),
            pl.BlockSpec((None, 3, D_CONV), lsel),
            pl.BlockSpec((None, 4, 3 * D_GDN), lsel),
            pl.BlockSpec((None, 1, LANES), lsel),
            pl.BlockSpec((None, 1, LANES), lsel),
            pl.BlockSpec((None, 1, HEAD_DIM), lsel),
        ],
        out_specs=[
            pl.BlockSpec((ns, CHUNK, D_CONV), lambda g, t: (g, t, 0)),
            pl.BlockSpec((ns, CHUNK, D_GDN), prev),
            pl.BlockSpec((None, ns, 2, D_CONV), lambda g, t: (layer, g, 0, 0)),
            pl.BlockSpec((None, ns, 3, 3 * D_GDN), lambda g, t: (layer, g, 0, 0)),
            pl.BlockSpec((None, ns, HEADS, HEAD_DIM, HEAD_DIM), lambda g, t: (layer, g, 0, 0, 0)),
        ],
        out_shape=[
            jax.ShapeDtypeStruct((batch, seq + CHUNK, D_CONV), BF16),
            jax.ShapeDtypeStruct((batch, seq, D_GDN), BF16),
            jax.ShapeDtypeStruct((DEPTH, batch, 2, D_CONV), F32),
            jax.ShapeDtypeStruct((DEPTH, batch, 3, 3 * D_GDN), F32),
            jax.ShapeDtypeStruct((DEPTH, batch, HEADS, HEAD_DIM, HEAD_DIM), F32),
        ],
        scratch_shapes=[
            pltpu.VMEM((ns, HALO + CHUNK, D_CONV), F32),
            pltpu.VMEM((ns, HALO + CHUNK, 3 * D_GDN), F32),
            pltpu.VMEM((ns * HEADS, HEAD_DIM, HEAD_DIM), F32),
            *scratch, *scratch,
        ],
        compiler_params=pltpu.CompilerParams(
            dimension_semantics=("arbitrary", "arbitrary"), vmem_limit_bytes=VMEM_LIMIT),
        name="mixer_prompt",
    )(*chained, z.reshape(batch, seq, D_MAIN), zt.reshape(batch, seq, LANES), conv_a_w, conv_qkv_w,
      alog_rows, dtb_rows, onorm3)


def _mixer_sample_kernel(z_ref, zt_ref, sa_ref, sq_ref, s_ref, wa_ref, wq_ref, alog_ref, dtb_ref,
                         on_ref, y_ref, na_ref, nq_ref, ns_ref):
    nb = z_ref.shape[0]

    ch = z_ref[:, OFF_CA:OFF_CA + D_CONV] * z_ref[:, OFF_HA:OFF_HA + D_CONV]
    prev0 = sa_ref[:, 0:D_CONV]
    prev1 = sa_ref[:, D_CONV:2 * D_CONV]
    conv = prev0 * wa_ref[0:1, :] + prev1 * wa_ref[1:2, :] + ch * wa_ref[2:3, :]
    y_ref[:, 0:D_CONV] = (z_ref[:, OFF_BA:OFF_BA + D_CONV] * conv
                          * z_ref[:, OFF_GA:OFF_GA + D_CONV])
    na_ref[:, 0:D_CONV] = prev1
    na_ref[:, D_CONV:2 * D_CONV] = ch

    w3 = 3 * D_GDN
    nq_ref[0] = sq_ref[1]
    nq_ref[1] = sq_ref[2]
    nq_ref[2] = z_ref[:, OFF_QKV:OFF_QKV + w3]

    beta_all, g_all = _gate_terms(zt_ref[...], alog_ref[...], dtb_ref[...])
    eg_all = jnp.exp(g_all)
    rowid = lax.broadcasted_iota(jnp.int32, (nb, HEAD_DIM), 0)

    def conv_qkv(off):
        sl = slice(off, off + HEAD_DIM)
        acc = z_ref[:, OFF_QKV + off:OFF_QKV + off + HEAD_DIM] * wq_ref[3:4, sl]
        for j in range(3):
            acc = acc + sq_ref[j, :, sl] * wq_ref[j:j + 1, sl]
        return _silu_of_half(acc)

    qn, kn, vv, betas, egs = [], [], [], [], []
    for h in range(HEADS):
        q = conv_qkv(h * HEAD_DIM)
        k = conv_qkv(D_GDN + h * HEAD_DIM)
        v = conv_qkv(2 * D_GDN + h * HEAD_DIM)
        qn.append(q * (lax.rsqrt(jnp.sum(q * q, axis=-1, keepdims=True) + EPS) * (HEAD_DIM ** -0.5)))
        kn.append(k * lax.rsqrt(jnp.sum(k * k, axis=-1, keepdims=True) + EPS))
        vv.append(v)
        betas.append(beta_all[:, h:h + 1])
        egs.append(eg_all[:, HEADS + h:HEADS + h + 1])

    pairs = [(j, h) for j in range(nb) for h in range(HEADS)]
    s_old = s_ref[...].reshape(nb * HEADS, HEAD_DIM, HEAD_DIM)
    kq = jnp.stack([jnp.concatenate([kn[h], qn[h]], axis=0) for _, h in pairs], axis=0)
    r = _bdot(kq, s_old)
    us = []
    for h in range(HEADS):
        ks = jnp.zeros((nb, HEAD_DIM), F32)
        qs = jnp.zeros((nb, HEAD_DIM), F32)
        for j in range(nb):
            rp = r[j * HEADS + h]
            ks = jnp.where(rowid == j, rp[:nb], ks)
            qs = jnp.where(rowid == j, rp[nb:], qs)
        u = betas[h] * (vv[h] - egs[h] * ks)
        o = egs[h] * qs + jnp.sum(qn[h] * kn[h], axis=-1, keepdims=True) * u
        o = o * lax.rsqrt(jnp.mean(o * o, axis=-1, keepdims=True) + EPS) * on_ref[...]
        gb = z_ref[:, OFF_GB + h * HEAD_DIM:OFF_GB + (h + 1) * HEAD_DIM]
        y_ref[:, D_CONV + h * HEAD_DIM:D_CONV + (h + 1) * HEAD_DIM] = o * gb
        us.append(u)
    k_own = jnp.stack([jnp.where(rowid == j, kn[h], 0.0) for j, h in pairs], axis=0)
    u_all = jnp.stack([us[h] for _, h in pairs], axis=0)
    decay = jnp.stack([jnp.broadcast_to(egs[h][j:j + 1, :], (1, HEAD_DIM))
                       for j, h in pairs], axis=0)
    s_new = s_old * decay + _bdot_tn(k_own, u_all)
    ns_ref[...] = s_new.reshape(ns_ref.shape)


def _mixer_sample_call(layer, z, zt, sa_all, sq_all, state_ssm, conv_a_w, conv_qkv_w, alog_rows,
                       dtb_rows, onorm3, stacks):
    nseq = z.shape[0]
    nb = SAMPLE_SEQS_PER_STEP
    lsel = lambda i: (layer, 0, 0)
    chained = () if stacks is None else tuple(stacks)
    n_chain = len(chained)
    return pl.pallas_call(
        functools.partial(_skip_refs, n_chain, _mixer_sample_kernel),
        grid=(nseq // nb,),
        input_output_aliases={i: 1 + i for i in range(n_chain)},
        in_specs=[pl.BlockSpec(memory_space=pl.ANY)] * n_chain + [
            pl.BlockSpec((nb, D_MAIN), lambda i: (i, 0)),
            pl.BlockSpec((nb, LANES), lambda i: (i, 0)),
            pl.BlockSpec((None, nb, 2 * D_CONV), lambda i: (layer, i, 0)),
            pl.BlockSpec((None, 3, nb, 3 * D_GDN), lambda i: (layer, 0, i, 0)),
            pl.BlockSpec((None, nb, HEADS, HEAD_DIM, HEAD_DIM), lambda i: (layer, i, 0, 0, 0)),
            pl.BlockSpec((None, 3, D_CONV), lsel),
            pl.BlockSpec((None, 4, 3 * D_GDN), lsel),
            pl.BlockSpec((None, 1, LANES), lsel),
            pl.BlockSpec((None, 1, LANES), lsel),
            pl.BlockSpec((None, 1, HEAD_DIM), lsel),
        ],
        out_specs=[
            pl.BlockSpec((nb, D_MODEL), lambda i: (i, 0)),
            pl.BlockSpec((None, nb, 2 * D_CONV), lambda i: (layer, i, 0)),
            pl.BlockSpec((None, 3, nb, 3 * D_GDN), lambda i: (layer, 0, i, 0)),
            pl.BlockSpec((None, nb, HEADS, HEAD_DIM, HEAD_DIM), lambda i: (layer, i, 0, 0, 0)),
        ],
        out_shape=[
            jax.ShapeDtypeStruct((nseq, D_MODEL), F32),
            jax.ShapeDtypeStruct((DEPTH, nseq, 2 * D_CONV), F32),
            jax.ShapeDtypeStruct((DEPTH, 3, nseq, 3 * D_GDN), F32),
            jax.ShapeDtypeStruct((DEPTH, nseq, HEADS, HEAD_DIM, HEAD_DIM), F32),
        ],
        compiler_params=pltpu.CompilerParams(
            dimension_semantics=("arbitrary",), vmem_limit_bytes=VMEM_LIMIT),
        name="mixer_sample",
    )(*chained, z, zt, sa_all, sq_all, state_ssm, conv_a_w, conv_qkv_w, alog_rows, dtb_rows, onorm3)


def kernel(x_prompt, x_sample, state_conv_a, state_conv_qkv, state_ssm, c_prompt, c_sample, norm_g, w_ada, b_ada, w_in, conv_a_w, conv_qkv_w, a_log, dt_bias, o_norm_g, w_out, final_norm_g):
    batch, seq, _ = x_prompt.shape
    nseq = x_sample.shape[0]
    assert x_sample.shape[1] == 1 and seq % CHUNK == 0

    w_in_t = jnp.swapaxes(w_in, 1, 2)
    w_main = _cast_transpose_call(w_in_t, D_MAIN, 1024)
    w_tail = w_in_t[:, D_MAIN:, :].astype(BF16)
    w_out16 = w_out.astype(BF16)
    norm_g3 = norm_g.reshape(DEPTH, 1, D_MODEL)
    onorm3 = o_norm_g.reshape(DEPTH, 1, HEAD_DIM)
    final_g = final_norm_g.reshape(1, D_MODEL)
    conv_qkv_half = 0.5 * conv_qkv_w
    pad_heads = ((0, 0), (HEADS, LANES - 2 * HEADS))
    alog_rows = jnp.pad(a_log, pad_heads).reshape(DEPTH, 1, LANES)
    dtb_rows = jnp.pad(dt_bias, pad_heads).reshape(DEPTH, 1, LANES)

    n_cond = nseq + batch
    n_cond_pad = -(-n_cond // 8) * 8
    c_all = jnp.concatenate(
        [c_sample, c_prompt, jnp.zeros((n_cond_pad - n_cond, D_MODEL), F32)], axis=0)
    mod = _mod_call(c_all, w_ada, b_ada)

    xp = x_prompt.reshape(batch * seq, D_MODEL)
    xs = x_sample.reshape(nseq, D_MODEL)
    tm_in = 1024
    tm_out = 1024
    sa_all = state_conv_a.reshape(DEPTH, nseq, 2 * D_CONV)
    sq_all = jnp.swapaxes(state_conv_qkv, 1, 2)
    prompt_states = sample_states = None

    for layer in range(DEPTH):
        def pmod(sec, per_batch, layer=layer):
            return pl.BlockSpec((None, 8, D_MODEL),
                                lambda i, *_: (layer, (nseq + i // per_batch) // 8, sec))

        def prow(per_batch):
            return lambda i: (nseq + i // per_batch) % 8
        z, zt = _inproj_call(layer, xp, norm_g3, mod, (pmod(1, seq // tm_in), pmod(0, seq // tm_in)),
                             prow(seq // tm_in), w_main, w_tail, tm_in)
        ya, yb, *prompt_states = _mixer_prompt_call(layer, z, zt, conv_a_w, conv_qkv_half, alog_rows,
                                                    dtb_rows, onorm3, batch, seq, prompt_states)
        per_seq = seq // tm_out
        y_specs = [pl.BlockSpec((None, tm_out, D_CONV), lambda i: (i // per_seq, i % per_seq, 0)),
                   pl.BlockSpec((None, tm_out, D_GDN), lambda i: (i // per_seq, i % per_seq, 0))]
        xp = _outproj_call(layer, ya, yb, y_specs, xp, mod, pmod(2, per_seq), prow(per_seq), w_out16,
                           final_g, tm_out)

        def smod(sec, layer=layer):
            return pl.BlockSpec((None, nseq, D_MODEL), lambda i, *_: (layer, 0, sec))
        z, zt = _inproj_call(layer, xs, norm_g3, mod, (smod(1), smod(0)), None, w_main, w_tail, nseq)
        y, *sample_states = _mixer_sample_call(layer, z, zt, sa_all, sq_all, state_ssm, conv_a_w,
                                               conv_qkv_half, alog_rows, dtb_rows, onorm3,
                                               sample_states)
        y_specs = [pl.BlockSpec((nseq, D_CONV), lambda i: (0, 0)),
                   pl.BlockSpec((nseq, D_GDN), lambda i: (0, 1))]
        xs = _outproj_call(layer, y, y, y_specs, xs, mod, smod(2), None, w_out16, final_g, nseq)

    ca_p, cq_p, ss_p = prompt_states
    na, nq, ss_s = sample_states
    return (xp.reshape(batch, seq, D_MODEL), xs.reshape(nseq, 1, D_MODEL), ca_p, cq_p, ss_p,
            na.reshape(DEPTH, nseq, 2, D_CONV), jnp.swapaxes(nq, 1, 2), ss_s)
```

```python
import functools
from typing import NamedTuple

import jax
import jax.numpy as jnp
from jax import lax
from jax.experimental import pallas as pl
from jax.experimental.pallas import tpu as pltpu

F32 = jnp.float32
BF16 = jnp.bfloat16

D_MODEL = 2048
DEPTH = 4
D_CONV = 1024
D_GDN = 1024
HEADS = 8
HEAD_DIM = 128
CHUNK = 64
EPS = 1e-6
D_MAIN = 8192
OFF_BA, OFF_CA, OFF_HA, OFF_GA, OFF_QKV, OFF_GB = 0, 1024, 2048, 3072, 4096, 7168
LANES = 128
HALO = 8
VMEM_LIMIT = 56 * 1024 * 1024
VMEM_LIMIT_BIG_TILES = 60 * 1024 * 1024
PROMPT_STREAMS = 2
SAMPLE_SEQS_PER_STEP = 16
PREP_TASKS_PER_GAP = (8, 5, 5, 5, 4, 3, 2, 2, 0, 0)


def _silu_of_half(hx):
    return hx + hx * jnp.tanh(hx)


def _silu(x):
    return _silu_of_half(0.5 * x)


def _softplus(x):
    return jnp.maximum(x, 0.0) + jnp.log1p(jnp.exp(-jnp.abs(x)))


def _dot(a, b):
    return jnp.dot(a, b, preferred_element_type=F32)


def _dot_nt(a, b):
    return lax.dot_general(a, b, (((1,), (1,)), ((), ())), preferred_element_type=F32)


def _dot_hi(a, b):
    return jnp.dot(a, b, preferred_element_type=F32, precision=lax.Precision.HIGHEST)


def _mod_kernel(c_ref, w_ref, b_ref, o_ref):
    s = _silu(c_ref[...]).astype(BF16)
    o_ref[...] = _dot(s, w_ref[...].astype(BF16)) + b_ref[...]


def _mod_call(c_all, w_ada, b_ada):
    rows = c_all.shape[0]
    tn = 1024
    return pl.pallas_call(
        _mod_kernel,
        grid=(DEPTH, 3 * D_MODEL // tn),
        in_specs=[
            pl.BlockSpec((rows, D_MODEL), lambda l, j: (0, 0)),
            pl.BlockSpec((None, D_MODEL, tn), lambda l, j: (l, 0, j)),
            pl.BlockSpec((None, 1, tn), lambda l, j: (l, 0, j)),
        ],
        out_specs=pl.BlockSpec((None, rows, tn), lambda l, j: (l, 0, j)),
        out_shape=jax.ShapeDtypeStruct((DEPTH, rows, 3 * D_MODEL), F32),
        compiler_params=pltpu.CompilerParams(
            dimension_semantics=("arbitrary", "arbitrary"), vmem_limit_bytes=VMEM_LIMIT),
        name="adaln_mod",
    )(c_all, w_ada, b_ada.reshape(DEPTH, 1, 3 * D_MODEL))


def _cast_transpose_kernel(w_ref, o_ref):
    o_ref[...] = w_ref[...].T.astype(BF16)


def _cast_transpose_call(wt, rows, tr):
    depth, _, cols = wt.shape
    assert rows % tr == 0
    return pl.pallas_call(
        _cast_transpose_kernel,
        grid=(depth, rows // tr),
        in_specs=[pl.BlockSpec((None, tr, cols), lambda l, i: (l, i, 0))],
        out_specs=pl.BlockSpec((None, cols, tr), lambda l, i: (l, 0, i)),
        out_shape=jax.ShapeDtypeStruct((depth, cols, rows), BF16),
        compiler_params=pltpu.CompilerParams(
            dimension_semantics=("arbitrary", "arbitrary"), vmem_limit_bytes=VMEM_LIMIT),
        name="cast_bf16",
    )(wt)


def _cond_rows(ref, cond_row):
    if cond_row is None:
        return ref[...]
    return ref[pl.ds(cond_row(pl.program_id(0)), 1), :]


def _inproj_kernel(x_ref, g_ref, sc_ref, sh_ref, w_ref, wt_ref, z_ref, zt_ref, h_scr, *, cond_row):
    @pl.when(pl.program_id(1) == 0)
    def _():
        x = x_ref[...]
        y = x * lax.rsqrt(jnp.mean(x * x, axis=-1, keepdims=True) + EPS) * g_ref[...]
        h = (y * (1.0 + _cond_rows(sc_ref, cond_row)) + _cond_rows(sh_ref, cond_row)).astype(BF16)
        h_scr[...] = h
        zt = _dot_nt(h, wt_ref[...])
        zt_ref[...] = jnp.concatenate(
            [zt, jnp.zeros((zt.shape[0], LANES - zt.shape[1]), F32)], axis=1)

    z = _dot(h_scr[...], w_ref[...])
    sections = z_ref.shape[1] // D_CONV
    for s in range(sections):
        sl = slice(s * D_CONV, (s + 1) * D_CONV)
        is_gate = (pl.program_id(1) * sections + s) % 4 == 3
        z_ref[:, sl] = jnp.where(is_gate, _silu(z[:, sl]), z[:, sl])


def _inproj_call(layer, x, norm_g3, mod_arr, mod_specs, cond_row, w_main, w_tail, tm):
    rows = x.shape[0]
    tn = 2 * D_CONV if tm >= 1024 else D_CONV
    sc_spec, sh_spec = mod_specs
    return pl.pallas_call(
        functools.partial(_inproj_kernel, cond_row=cond_row),
        grid=(rows // tm, D_MAIN // tn),
        in_specs=[
            pl.BlockSpec((tm, D_MODEL), lambda i, j: (i, 0)),
            pl.BlockSpec((None, 1, D_MODEL), lambda i, j: (layer, 0, 0)),
            sc_spec,
            sh_spec,
            pl.BlockSpec((None, D_MODEL, tn), lambda i, j: (layer, 0, j)),
            pl.BlockSpec((None, 2 * HEADS, D_MODEL), lambda i, j: (layer, 0, 0)),
        ],
        out_specs=[
            pl.BlockSpec((tm, tn), lambda i, j: (i, j)),
            pl.BlockSpec((tm, LANES), lambda i, j: (i, 0)),
        ],
        out_shape=[
            jax.ShapeDtypeStruct((rows, D_MAIN), F32),
            jax.ShapeDtypeStruct((rows, LANES), F32),
        ],
        scratch_shapes=[pltpu.VMEM((tm, D_MODEL), BF16)],
        compiler_params=pltpu.CompilerParams(
            dimension_semantics=("arbitrary", "arbitrary"), vmem_limit_bytes=VMEM_LIMIT_BIG_TILES),
        name="in_proj",
    )(x, norm_g3, mod_arr, mod_arr, w_main, w_tail)


def _outproj_kernel(ya_ref, yb_ref, x_ref, gate_ref, wa_ref, wb_ref, fg_ref, o_ref, *, final,
                    cond_row):
    acc = (_dot(ya_ref[...].astype(BF16), wa_ref[...])
           + _dot(yb_ref[...].astype(BF16), wb_ref[...]))
    xn = x_ref[...] + _cond_rows(gate_ref, cond_row) * acc
    if final:
        xn = xn * lax.rsqrt(jnp.mean(xn * xn, axis=-1, keepdims=True) + EPS) * fg_ref[...]
    o_ref[...] = xn


def _outproj_call(layer, ya, yb, y_specs, x, mod_arr, gate_spec, cond_row, w_out, final_g, tm):
    rows = x.shape[0]
    half = D_MODEL // 2
    return pl.pallas_call(
        functools.partial(_outproj_kernel, final=(layer == DEPTH - 1), cond_row=cond_row),
        grid=(rows // tm,),
        in_specs=[
            *y_specs,
            pl.BlockSpec((tm, D_MODEL), lambda i: (i, 0)),
            gate_spec,
            pl.BlockSpec((None, half, D_MODEL), lambda i: (layer, 0, 0), pipeline_mode=pl.Buffered(1)),
            pl.BlockSpec((None, half, D_MODEL), lambda i: (layer, 1, 0), pipeline_mode=pl.Buffered(1)),
            pl.BlockSpec((1, D_MODEL), lambda i: (0, 0)),
        ],
        out_specs=pl.BlockSpec((tm, D_MODEL), lambda i: (i, 0)),
        out_shape=jax.ShapeDtypeStruct((rows, D_MODEL), F32),
        compiler_params=pltpu.CompilerParams(
            dimension_semantics=("arbitrary",), vmem_limit_bytes=VMEM_LIMIT_BIG_TILES),
        name="out_proj",
    )(ya, yb, x, mod_arr, w_out, w_out, final_g)


def _gate_terms(tail, alog_row, dtb_row):
    beta = jax.nn.sigmoid(tail)
    g = -jnp.exp(alog_row) * _softplus(tail + dtb_row)
    return beta, g


def _bdot(a, b):
    return lax.dot_general(a, b, (((2,), (1,)), ((0,), (0,))), preferred_element_type=F32)


def _bdot_nt(a, b):
    return lax.dot_general(a, b, (((2,), (2,)), ((0,), (0,))), preferred_element_type=F32)


def _bdot_tn(a, b):
    return lax.dot_general(a, b, (((1,), (1,)), ((0,), (0,))), preferred_element_type=F32)


def _split_bf16(a):
    hi = a.astype(BF16)
    lo = (a - hi.astype(F32)).astype(BF16)
    return hi, lo


def _unit_lower_inverse_levels(lmat, eye):
    heads, c, _ = lmat.shape
    eye_b = jnp.broadcast_to(eye, lmat.shape)
    pass_x = jnp.concatenate([eye_b, jnp.zeros_like(eye_b)], axis=2).astype(BF16)
    zeros_top = jnp.zeros((heads, c, 2 * c), BF16)
    w = jnp.concatenate([eye_b, -lmat], axis=2)
    for _ in range(6):
        w_hi, w_lo = _split_bf16(w)
        lhs = jnp.concatenate([w_hi, w_lo, w_hi], axis=2)
        rhs = jnp.concatenate([pass_x, w_hi, pass_x, w_hi, zeros_top, w_lo], axis=1)
        w = _bdot(lhs, rhs)
        yield w[:, :, :c], w


class _ChunkOperands(NamedTuple):
    k16: object
    kb16: object
    q16: object
    rhs16: object
    qeg16: object
    kt16: object
    dec: object
    sdec: object
    gate: object


def _operand_scratch(ns):
    nh = ns * HEADS
    return _ChunkOperands(
        k16=pltpu.VMEM((nh, CHUNK, HEAD_DIM), BF16),
        kb16=pltpu.VMEM((nh, CHUNK, HEAD_DIM), BF16),
        q16=pltpu.VMEM((nh, CHUNK, HEAD_DIM), BF16),
        rhs16=pltpu.VMEM((nh, CHUNK, 2 * HEAD_DIM), BF16),
        qeg16=pltpu.VMEM((nh, CHUNK, HEAD_DIM), BF16),
        kt16=pltpu.VMEM((nh, CHUNK, HEAD_DIM), BF16),
        dec=pltpu.VMEM((nh, CHUNK, CHUNK), F32),
        sdec=pltpu.VMEM((nh, 1, HEAD_DIM), F32),
        gate=pltpu.VMEM((ns, CHUNK, D_GDN), F32),
    )


def _prep_tasks(z_ref, zt_ref, wa_ref, wq_ref, alog_ref, dtb_ref, ya_ref, ca_ref, cq_ref,
                buf_a, buf_q, out):
    C = CHUNK
    ns = z_ref.shape[0]
    row = lax.broadcasted_iota(jnp.int32, (C, C), 0)
    col = lax.broadcasted_iota(jnp.int32, (C, C), 1)
    causal = row >= col
    tri = causal.astype(F32)
    sub = lax.broadcasted_iota(jnp.int32, (C // 8, 8, HEAD_DIM), 1)
    never = pl.program_id(0) < 0

    def pinned(row_vec, token):
        return row_vec if token is None else jnp.where(never, token, row_vec)

    def shifted(x3, d):
        r = pltpu.roll(x3, d, axis=1)
        return jnp.where(sub < d, r[:-1], r[1:]).reshape(C, x3.shape[-1])

    def haloed(buf, s, sl):
        x = buf[s, :, sl]
        return x, x.reshape((HALO + C) // 8, 8, HEAD_DIM)

    gates = {}

    def gate_task(s):
        def run(token):
            del token
            beta_s, g_s = _gate_terms(zt_ref[s], alog_ref[...], dtb_ref[...])
            gc_s = _dot_hi(tri, g_s)
            gates[s] = (beta_s, gc_s, gc_s.T)
            buf_a[s, HALO:HALO + C, :] = (z_ref[s, :, OFF_CA:OFF_CA + D_CONV]
                                          * z_ref[s, :, OFF_HA:OFF_HA + D_CONV])
            ca_ref[s] = buf_a[s, HALO + C - 2:HALO + C, :]
            buf_q[s, HALO:HALO + C, :] = z_ref[s, :, OFF_QKV:OFF_QKV + 3 * D_GDN]
            cq_ref[s] = buf_q[s, HALO + C - 3:HALO + C, :]
        return run

    def conv_a_task(s, off):
        def run(token):
            sl = slice(off, off + HEAD_DIM)
            xa, xa3 = haloed(buf_a, s, sl)
            conv = (shifted(xa3, 2) * pinned(wa_ref[0:1, sl], token)
                    + shifted(xa3, 1) * pinned(wa_ref[1:2, sl], token)
                    + xa[HALO:] * pinned(wa_ref[2:3, sl], token))
            ya = (z_ref[s, :, OFF_BA + off:OFF_BA + off + HEAD_DIM] * conv
                  * z_ref[s, :, OFF_GA + off:OFF_GA + off + HEAD_DIM])
            ya_ref[s, :, sl] = ya.astype(ya_ref.dtype)
            buf_a[s, 0:HALO, sl] = xa[C:]
        return run

    def conv_qkv(s, off, token):
        sl = slice(off, off + HEAD_DIM)
        x, x3 = haloed(buf_q, s, sl)
        acc = x[HALO:] * pinned(wq_ref[3:4, sl], token)
        for d in range(1, 4):
            acc = acc + shifted(x3, d) * pinned(wq_ref[3 - d:4 - d, sl], token)
        buf_q[s, 0:HALO, sl] = x[C:]
        return _silu_of_half(acc)

    def head_task(s, h):
        def run(token):
            n = s * HEADS + h
            beta_s, gc_s, gc_t = gates[s]
            sl = slice(h * HEAD_DIM, (h + 1) * HEAD_DIM)
            q = conv_qkv(s, h * HEAD_DIM, token)
            k = conv_qkv(s, D_GDN + h * HEAD_DIM, token)
            v = conv_qkv(s, 2 * D_GDN + h * HEAD_DIM, token)
            q = q * (lax.rsqrt(jnp.sum(q * q, axis=-1, keepdims=True) + EPS) * (HEAD_DIM ** -0.5))
            k = k * lax.rsqrt(jnp.sum(k * k, axis=-1, keepdims=True) + EPS)
            beta = jnp.broadcast_to(beta_s[:, h:h + 1], (C, HEAD_DIM))
            gcol = jnp.broadcast_to(gc_s[:, HEADS + h:HEADS + h + 1], (C, HEAD_DIM))
            grow = gc_t[HEADS + h:HEADS + h + 1, :]
            glast = gcol[C - 1:C, :]
            eg = jnp.exp(gcol)
            kb = k * beta
            out.k16[n] = k.astype(BF16)
            out.kb16[n] = kb.astype(BF16)
            out.q16[n] = q.astype(BF16)
            out.rhs16[n] = jnp.concatenate([v * beta, kb * eg], axis=1).astype(BF16)
            out.qeg16[n] = (q * eg).astype(BF16)
            out.kt16[n] = (k * jnp.exp(glast - gcol)).astype(BF16)
            out.dec[n] = jnp.exp(jnp.where(causal, gcol[:, :C] - grow, -jnp.inf))
            out.sdec[n] = jnp.exp(glast)
            out.gate[s, :, sl] = z_ref[s, :, OFF_GB + h * HEAD_DIM:OFF_GB + (h + 1) * HEAD_DIM]
        return run

    tasks = [gate_task(s) for s in range(ns)]
    for s in range(ns):
        for h in range(HEADS):
            tasks.append(head_task(s, h))
            tasks.append(conv_a_task(s, h * HEAD_DIM))
    return tasks


def _scan_stages(ops, on_ref, yb_ref, s_scr):
    C = CHUNK
    ns = yb_ref.shape[0]
    row = lax.broadcasted_iota(jnp.int32, (C, C), 0)
    col = lax.broadcasted_iota(jnp.int32, (C, C), 1)
    strict = (row > col)[None]
    eye = (row == col).astype(F32)[None]

    k16 = ops.k16[...]
    dec_causal = ops.dec[...]
    lmat = _bdot_nt(ops.kb16[...], k16) * jnp.where(strict, dec_causal, 0.0)
    attn = (_bdot_nt(ops.q16[...], k16) * dec_causal).astype(BF16)
    nh = k16.shape[0]
    token_of = lambda a: a[nh - 1, C - 1:C, :HEAD_DIM]
    yield None
    for tmat, w_level in _unit_lower_inverse_levels(lmat, eye):
        yield token_of(w_level)
    uw = _bdot(tmat.astype(BF16), ops.rhs16[...])
    yield token_of(uw)
    s_old = s_scr[...]
    lhs = jnp.concatenate([uw[:, :, HEAD_DIM:].astype(BF16), ops.qeg16[...]], axis=1)
    ws = _bdot(lhs, s_old.astype(BF16))
    yield token_of(ws)
    u = uw[:, :, :HEAD_DIM] - ws[:, :C]
    u16 = u.astype(BF16)
    o = ws[:, C:] + _bdot(attn, u16)
    s_scr[...] = s_old * ops.sdec[...] + _bdot_tn(ops.kt16[...], u16)
    yield token_of(o)
    o = o * lax.rsqrt(jnp.mean(o * o, axis=-1, keepdims=True) + EPS) * on_ref[...]
    for s in range(ns):
        for h in range(HEADS):
            sl = slice(h * HEAD_DIM, (h + 1) * HEAD_DIM)
            yb_ref[s, :, sl] = (o[s * HEADS + h] * ops.gate[s, :, sl]).astype(yb_ref.dtype)


def _interleave(stages, tasks, per_gap):
    todo = list(tasks)
    token = None
    for n, token in zip(list(per_gap) + [0] * 64, stages):
        for task in todo[:n]:
            task(token)
        todo = todo[n:]
    for task in todo:
        task(token)


def _mixer_prompt_kernel(z_ref, zt_ref, wa_ref, wq_ref, alog_ref, dtb_ref, on_ref,
                         ya_ref, yb_ref, ca_ref, cq_ref, ss_ref, buf_a, buf_q, s_scr, *operand_refs,
                         chunks_per_seq):
    t = pl.program_id(0)
    ns = z_ref.shape[0]
    n_fields = len(_ChunkOperands._fields)
    sets = (_ChunkOperands(*operand_refs[:n_fields]), _ChunkOperands(*operand_refs[n_fields:]))
    first_of_group = t % chunks_per_seq == 0

    @pl.when(t == 0)
    def _():
        for ref in sets[1]:
            ref[...] = jnp.zeros(ref.shape, ref.dtype)

    @pl.when(first_of_group)
    def _():
        buf_a[:, 0:HALO, :] = jnp.zeros((ns, HALO, D_CONV), F32)
        buf_q[:, 0:HALO, :] = jnp.zeros((ns, HALO, 3 * D_GDN), F32)

    @pl.when((t <= 1) | (t % chunks_per_seq == 1))
    def _():
        s_scr[...] = jnp.zeros(s_scr.shape, F32)

    for parity in range(2):
        @pl.when(t % 2 == parity)
        def _(parity=parity):
            _interleave(
                _scan_stages(sets[1 - parity], on_ref, yb_ref, s_scr),
                _prep_tasks(z_ref, zt_ref, wa_ref, wq_ref, alog_ref, dtb_ref, ya_ref, ca_ref,
                            cq_ref, buf_a, buf_q, sets[parity]),
                PREP_TASKS_PER_GAP)

    @pl.when(first_of_group & (t > 0))
    def _():
        ss_ref[...] = s_scr[...].reshape(ss_ref.shape)


def _skip_refs(n, kernel, *refs):
    kernel(*refs[n:])


def _mixer_prompt_call(layer, z, zt, conv_a_w, conv_qkv_w, alog_rows, dtb_rows, onorm3, batch, seq,
                       stacks):
    nc = seq // CHUNK
    ns = PROMPT_STREAMS
    n_pairs = (batch // ns) * nc
    lsel = lambda t: (layer, 0, 0)
    prep_pair = lambda t: jnp.minimum(t, n_pairs - 1)
    scan_pair = lambda t: jnp.maximum(t - 1, 0)
    cur = lambda t: (prep_pair(t) // nc, prep_pair(t) % nc, 0)
    prev = lambda t: (scan_pair(t) // nc, scan_pair(t) % nc, 0)
    scratch = _operand_scratch(ns)
    chained = () if stacks is None else tuple(stacks)
    n_chain = len(chained)
    return pl.pallas_call(
        functools.partial(_skip_refs, n_chain,
                          functools.partial(_mixer_prompt_kernel, chunks_per_seq=nc)),
        grid=(n_pairs + 1,),
        input_output_aliases={i: 2 + i for i in range(n_chain)},
        in_specs=[pl.BlockSpec(memory_space=pl.ANY)] * n_chain + [
            pl.BlockSpec((ns, CHUNK, D_MAIN), cur),
            pl.BlockSpec((ns, CHUNK, LANES), cur),
            pl.BlockSpec((None, 3, D_CONV), lsel),
            pl.BlockSpec((None, 4, 3 * D_GDN), lsel),
            pl.BlockSpec((None, 1, LANES), lsel),
            pl.BlockSpec((None, 1, LANES), lsel),
            pl.BlockSpec((None, 1, HEAD_DIM), lsel),
        ],
        out_specs=[
            pl.BlockSpec((ns, CHUNK, D_CONV),
                         lambda t: (prep_pair(t) // nc, t - (prep_pair(t) // nc) * nc, 0)),
            pl.BlockSpec((ns, CHUNK, D_GDN), prev),
            pl.BlockSpec((None, ns, 2, D_CONV), lambda t: (layer, prep_pair(t) // nc, 0, 0)),
            pl.BlockSpec((None, ns, 3, 3 * D_GDN), lambda t: (layer, prep_pair(t) // nc, 0, 0)),
            pl.BlockSpec((None, ns, HEADS, HEAD_DIM, HEAD_DIM),
                         lambda t: (layer, scan_pair(t) // nc, 0, 0, 0)),
        ],
        out_shape=[
            jax.ShapeDtypeStruct((batch, seq + CHUNK, D_CONV), BF16),
            jax.ShapeDtypeStruct((batch, seq, D_GDN), BF16),
            jax.ShapeDtypeStruct((DEPTH, batch, 2, D_CONV), F32),
            jax.ShapeDtypeStruct((DEPTH, batch, 3, 3 * D_GDN), F32),
            jax.ShapeDtypeStruct((DEPTH, batch, HEADS, HEAD_DIM, HEAD_DIM), F32),
        ],
        scratch_shapes=[
            pltpu.VMEM((ns, HALO + CHUNK, D_CONV), F32),
            pltpu.VMEM((ns, HALO + CHUNK, 3 * D_GDN), F32),
            pltpu.VMEM((ns * HEADS, HEAD_DIM, HEAD_DIM), F32),
            *scratch, *scratch,
        ],
        compiler_params=pltpu.CompilerParams(
            dimension_semantics=("arbitrary",), vmem_limit_bytes=VMEM_LIMIT),
        name="mixer_prompt",
    )(*chained, z.reshape(batch, seq, D_MAIN), zt.reshape(batch, seq, LANES), conv_a_w, conv_qkv_w,
      alog_rows, dtb_rows, onorm3)


def _mixer_sample_kernel(z_ref, zt_ref, sa_ref, sq_ref, s_ref, wa_ref, wq_ref, alog_ref, dtb_ref,
                         on_ref, y_ref, na_ref, nq_ref, ns_ref):
    nb = z_ref.shape[0]

    ch = z_ref[:, OFF_CA:OFF_CA + D_CONV] * z_ref[:, OFF_HA:OFF_HA + D_CONV]
    prev0 = sa_ref[:, 0:D_CONV]
    prev1 = sa_ref[:, D_CONV:2 * D_CONV]
    conv = prev0 * wa_ref[0:1, :] + prev1 * wa_ref[1:2, :] + ch * wa_ref[2:3, :]
    y_ref[:, 0:D_CONV] = (z_ref[:, OFF_BA:OFF_BA + D_CONV] * conv
                          * z_ref[:, OFF_GA:OFF_GA + D_CONV])
    na_ref[:, 0:D_CONV] = prev1
    na_ref[:, D_CONV:2 * D_CONV] = ch

    w3 = 3 * D_GDN
    nq_ref[0] = sq_ref[1]
    nq_ref[1] = sq_ref[2]
    nq_ref[2] = z_ref[:, OFF_QKV:OFF_QKV + w3]

    beta_all, g_all = _gate_terms(zt_ref[...], alog_ref[...], dtb_ref[...])
    eg_all = jnp.exp(g_all)
    rowid = lax.broadcasted_iota(jnp.int32, (nb, HEAD_DIM), 0)

    def conv_qkv(off):
        sl = slice(off, off + HEAD_DIM)
        acc = z_ref[:, OFF_QKV + off:OFF_QKV + off + HEAD_DIM] * wq_ref[3:4, sl]
        for j in range(3):
            acc = acc + sq_ref[j, :, sl] * wq_ref[j:j + 1, sl]
        return _silu_of_half(acc)

    qn, kn, vv, betas, egs = [], [], [], [], []
    for h in range(HEADS):
        q = conv_qkv(h * HEAD_DIM)
        k = conv_qkv(D_GDN + h * HEAD_DIM)
        v = conv_qkv(2 * D_GDN + h * HEAD_DIM)
        qn.append(q * (lax.rsqrt(jnp.sum(q * q, axis=-1, keepdims=True) + EPS) * (HEAD_DIM ** -0.5)))
        kn.append(k * lax.rsqrt(jnp.sum(k * k, axis=-1, keepdims=True) + EPS))
        vv.append(v)
        betas.append(beta_all[:, h:h + 1])
        egs.append(eg_all[:, HEADS + h:HEADS + h + 1])

    pairs = [(j, h) for j in range(nb) for h in range(HEADS)]
    s_old = s_ref[...].reshape(nb * HEADS, HEAD_DIM, HEAD_DIM)
    kq = jnp.stack([jnp.concatenate([kn[h], qn[h]], axis=0) for _, h in pairs], axis=0)
    r = _bdot(kq, s_old)
    us = []
    for h in range(HEADS):
        ks = jnp.zeros((nb, HEAD_DIM), F32)
        qs = jnp.zeros((nb, HEAD_DIM), F32)
        for j in range(nb):
            rp = r[j * HEADS + h]
            ks = jnp.where(rowid == j, rp[:nb], ks)
            qs = jnp.where(rowid == j, rp[nb:], qs)
        u = betas[h] * (vv[h] - egs[h] * ks)
        o = egs[h] * qs + jnp.sum(qn[h] * kn[h], axis=-1, keepdims=True) * u
        o = o * lax.rsqrt(jnp.mean(o * o, axis=-1, keepdims=True) + EPS) * on_ref[...]
        gb = z_ref[:, OFF_GB + h * HEAD_DIM:OFF_GB + (h + 1) * HEAD_DIM]
        y_ref[:, D_CONV + h * HEAD_DIM:D_CONV + (h + 1) * HEAD_DIM] = o * gb
        us.append(u)
    k_own = jnp.stack([jnp.where(rowid == j, kn[h], 0.0) for j, h in pairs], axis=0)
    u_all = jnp.stack([us[h] for _, h in pairs], axis=0)
    decay = jnp.stack([jnp.broadcast_to(egs[h][j:j + 1, :], (1, HEAD_DIM))
                       for j, h in pairs], axis=0)
    s_new = s_old * decay + _bdot_tn(k_own, u_all)
    ns_ref[...] = s_new.reshape(ns_ref.shape)


def _mixer_sample_call(layer, z, zt, sa_all, sq_all, state_ssm, conv_a_w, conv_qkv_w, alog_rows,
                       dtb_rows, onorm3, stacks):
    nseq = z.shape[0]
    nb = SAMPLE_SEQS_PER_STEP
    lsel = lambda i: (layer, 0, 0)
    chained = () if stacks is None else tuple(stacks)
    n_chain = len(chained)
    return pl.pallas_call(
        functools.partial(_skip_refs, n_chain, _mixer_sample_kernel),
        grid=(nseq // nb,),
        input_output_aliases={i: 1 + i for i in range(n_chain)},
        in_specs=[pl.BlockSpec(memory_space=pl.ANY)] * n_chain + [
            pl.BlockSpec((nb, D_MAIN), lambda i: (i, 0)),
            pl.BlockSpec((nb, LANES), lambda i: (i, 0)),
            pl.BlockSpec((None, nb, 2 * D_CONV), lambda i: (layer, i, 0)),
            pl.BlockSpec((None, 3, nb, 3 * D_GDN), lambda i: (layer, 0, i, 0)),
            pl.BlockSpec((None, nb, HEADS, HEAD_DIM, HEAD_DIM), lambda i: (layer, i, 0, 0, 0)),
            pl.BlockSpec((None, 3, D_CONV), lsel),
            pl.BlockSpec((None, 4, 3 * D_GDN), lsel),
            pl.BlockSpec((None, 1, LANES), lsel),
            pl.BlockSpec((None, 1, LANES), lsel),
            pl.BlockSpec((None, 1, HEAD_DIM), lsel),
        ],
        out_specs=[
            pl.BlockSpec((nb, D_MODEL), lambda i: (i, 0)),
            pl.BlockSpec((None, nb, 2 * D_CONV), lambda i: (layer, i, 0)),
            pl.BlockSpec((None, 3, nb, 3 * D_GDN), lambda i: (layer, 0, i, 0)),
            pl.BlockSpec((None, nb, HEADS, HEAD_DIM, HEAD_DIM), lambda i: (layer, i, 0, 0, 0)),
        ],
        out_shape=[
            jax.ShapeDtypeStruct((nseq, D_MODEL), F32),
            jax.ShapeDtypeStruct((DEPTH, nseq, 2 * D_CONV), F32),
            jax.ShapeDtypeStruct((DEPTH, 3, nseq, 3 * D_GDN), F32),
            jax.ShapeDtypeStruct((DEPTH, nseq, HEADS, HEAD_DIM, HEAD_DIM), F32),
        ],
        compiler_params=pltpu.CompilerParams(
            dimension_semantics=("arbitrary",), vmem_limit_bytes=VMEM_LIMIT),
        name="mixer_sample",
    )(*chained, z, zt, sa_all, sq_all, state_ssm, conv_a_w, conv_qkv_w, alog_rows, dtb_rows, onorm3)


def kernel(x_prompt, x_sample, state_conv_a, state_conv_qkv, state_ssm, c_prompt, c_sample, norm_g, w_ada, b_ada, w_in, conv_a_w, conv_qkv_w, a_log, dt_bias, o_norm_g, w_out, final_norm_g):
    batch, seq, _ = x_prompt.shape
    nseq = x_sample.shape[0]
    assert x_sample.shape[1] == 1 and seq % CHUNK == 0

    w_in_t = jnp.swapaxes(w_in, 1, 2)
    w_main = _cast_transpose_call(w_in_t, D_MAIN, 1024)
    w_tail = w_in_t[:, D_MAIN:, :].astype(BF16)
    w_out16 = w_out.astype(BF16)
    norm_g3 = norm_g.reshape(DEPTH, 1, D_MODEL)
    onorm3 = o_norm_g.reshape(DEPTH, 1, HEAD_DIM)
    final_g = final_norm_g.reshape(1, D_MODEL)
    conv_qkv_half = 0.5 * conv_qkv_w
    pad_heads = ((0, 0), (HEADS, LANES - 2 * HEADS))
    alog_rows = jnp.pad(a_log, pad_heads).reshape(DEPTH, 1, LANES)
    dtb_rows = jnp.pad(dt_bias, pad_heads).reshape(DEPTH, 1, LANES)

    n_cond = nseq + batch
    n_cond_pad = -(-n_cond // 8) * 8
    c_all = jnp.concatenate(
        [c_sample, c_prompt, jnp.zeros((n_cond_pad - n_cond, D_MODEL), F32)], axis=0)
    mod = _mod_call(c_all, w_ada, b_ada)

    xp = x_prompt.reshape(batch * seq, D_MODEL)
    xs = x_sample.reshape(nseq, D_MODEL)
    tm_in = 1024
    tm_out = 1024
    sa_all = state_conv_a.reshape(DEPTH, nseq, 2 * D_CONV)
    sq_all = jnp.swapaxes(state_conv_qkv, 1, 2)
    prompt_states = sample_states = None

    for layer in range(DEPTH):
        def pmod(sec, per_batch, layer=layer):
            return pl.BlockSpec((None, 8, D_MODEL),
                                lambda i, *_: (layer, (nseq + i // per_batch) // 8, sec))

        def prow(per_batch):
            return lambda i: (nseq + i // per_batch) % 8
        z, zt = _inproj_call(layer, xp, norm_g3, mod, (pmod(1, seq // tm_in), pmod(0, seq // tm_in)),
                             prow(seq // tm_in), w_main, w_tail, tm_in)
        ya, yb, *prompt_states = _mixer_prompt_call(layer, z, zt, conv_a_w, conv_qkv_half, alog_rows,
                                                    dtb_rows, onorm3, batch, seq, prompt_states)
        per_seq = seq // tm_out
        y_specs = [pl.BlockSpec((None, tm_out, D_CONV), lambda i: (i // per_seq, i % per_seq, 0)),
                   pl.BlockSpec((None, tm_out, D_GDN), lambda i: (i // per_seq, i % per_seq, 0))]
        xp = _outproj_call(layer, ya, yb, y_specs, xp, mod, pmod(2, per_seq), prow(per_seq), w_out16,
                           final_g, tm_out)

        def smod(sec, layer=layer):
            return pl.BlockSpec((None, nseq, D_MODEL), lambda i, *_: (layer, 0, sec))
        z, zt = _inproj_call(layer, xs, norm_g3, mod, (smod(1), smod(0)), None, w_main, w_tail, nseq)
        y, *sample_states = _mixer_sample_call(layer, z, zt, sa_all, sq_all, state_ssm, conv_a_w,
                                               conv_qkv_half, alog_rows, dtb_rows, onorm3,
                                               sample_states)
        y_specs = [pl.BlockSpec((nseq, D_CONV), lambda i: (0, 0)),
                   pl.BlockSpec((nseq, D_GDN), lambda i: (0, 1))]
        xs = _outproj_call(layer, y, y, y_specs, xs, mod, smod(2), None, w_out16, final_g, nseq)

    ca_p, cq_p, ss_p = prompt_states
    na, nq, ss_s = sample_states
    return (xp.reshape(batch, seq, D_MODEL), xs.reshape(nseq, 1, D_MODEL), ca_p, cq_p, ss_p,
            na.reshape(DEPTH, nseq, 2, D_CONV), jnp.swapaxes(nq, 1, 2), ss_s)
```

```python
import functools
from typing import NamedTuple

import jax
import jax.numpy as jnp
from jax import lax
from jax.experimental import pallas as pl
from jax.experimental.pallas import tpu as pltpu

F32 = jnp.float32
BF16 = jnp.bfloat16

D_MODEL = 2048
DEPTH = 4
D_CONV = 1024
D_GDN = 1024
HEADS = 8
HEAD_DIM = 128
CHUNK = 64
EPS = 1e-6
D_MAIN = 8192
OFF_BA, OFF_CA, OFF_HA, OFF_GA, OFF_QKV, OFF_GB = 0, 1024, 2048, 3072, 4096, 7168
LANES = 128
HALO = 8
VMEM_LIMIT = 56 * 1024 * 1024
VMEM_LIMIT_BIG_TILES = 60 * 1024 * 1024
PROMPT_STREAMS = 2
SAMPLE_SEQS_PER_STEP = 16
PREP_TASKS_PER_GAP = (8, 5, 5, 5, 4, 3, 2, 2, 0, 0)


def _silu_of_half(hx):
    return hx + hx * jnp.tanh(hx)


def _silu(x):
    return _silu_of_half(0.5 * x)


def _softplus(x):
    return jnp.maximum(x, 0.0) + jnp.log1p(jnp.exp(-jnp.abs(x)))


def _dot(a, b):
    return jnp.dot(a, b, preferred_element_type=F32)


def _dot_nt(a, b):
    return lax.dot_general(a, b, (((1,), (1,)), ((), ())), preferred_element_type=F32)


def _dot_hi(a, b):
    return jnp.dot(a, b, preferred_element_type=F32, precision=lax.Precision.HIGHEST)


def _mod_kernel(c_ref, w_ref, b_ref, o_ref):
    s = _silu(c_ref[...]).astype(BF16)
    o_ref[...] = _dot(s, w_ref[...].astype(BF16)) + b_ref[...]


def _mod_call(c_all, w_ada, b_ada):
    rows = c_all.shape[0]
    tn = 2048
    return pl.pallas_call(
        _mod_kernel,
        grid=(DEPTH, 3 * D_MODEL // tn),
        in_specs=[
            pl.BlockSpec((rows, D_MODEL), lambda l, j: (0, 0)),
            pl.BlockSpec((None, D_MODEL, tn), lambda l, j: (l, 0, j)),
            pl.BlockSpec((None, 1, tn), lambda l, j: (l, 0, j)),
        ],
        out_specs=pl.BlockSpec((None, rows, tn), lambda l, j: (l, 0, j)),
        out_shape=jax.ShapeDtypeStruct((DEPTH, rows, 3 * D_MODEL), F32),
        compiler_params=pltpu.CompilerParams(
            dimension_semantics=("arbitrary", "arbitrary"), vmem_limit_bytes=VMEM_LIMIT),
        name="adaln_mod",
    )(c_all, w_ada, b_ada.reshape(DEPTH, 1, 3 * D_MODEL))


def _cast_transpose_kernel(w_ref, o_ref):
    o_ref[...] = w_ref[...].T.astype(BF16)


def _cast_transpose_call(wt, rows, tr):
    depth, _, cols = wt.shape
    assert rows % tr == 0
    return pl.pallas_call(
        _cast_transpose_kernel,
        grid=(depth, rows // tr),
        in_specs=[pl.BlockSpec((None, tr, cols), lambda l, i: (l, i, 0))],
        out_specs=pl.BlockSpec((None, cols, tr), lambda l, i: (l, 0, i)),
        out_shape=jax.ShapeDtypeStruct((depth, cols, rows), BF16),
        compiler_params=pltpu.CompilerParams(
            dimension_semantics=("arbitrary", "arbitrary"), vmem_limit_bytes=VMEM_LIMIT),
        name="cast_bf16",
    )(wt)


def _cond_rows(ref, cond_row):
    if cond_row is None:
        return ref[...]
    return ref[pl.ds(cond_row(pl.program_id(0)), 1), :]


def _inproj_kernel(x_ref, g_ref, sc_ref, sh_ref, w_ref, wt_ref, z_ref, zt_ref, h_scr, *, cond_row):
    @pl.when(pl.program_id(1) == 0)
    def _():
        x = x_ref[...]
        y = x * lax.rsqrt(jnp.mean(x * x, axis=-1, keepdims=True) + EPS) * g_ref[...]
        h = (y * (1.0 + _cond_rows(sc_ref, cond_row)) + _cond_rows(sh_ref, cond_row)).astype(BF16)
        h_scr[...] = h
        zt = _dot_nt(h, wt_ref[...])
        zt_ref[...] = jnp.concatenate(
            [zt, jnp.zeros((zt.shape[0], LANES - zt.shape[1]), F32)], axis=1)

    z = _dot(h_scr[...], w_ref[...])
    sections = z_ref.shape[1] // D_CONV
    for s in range(sections):
        sl = slice(s * D_CONV, (s + 1) * D_CONV)
        is_gate = (pl.program_id(1) * sections + s) % 4 == 3
        z_ref[:, sl] = jnp.where(is_gate, _silu(z[:, sl]), z[:, sl])


def _inproj_call(layer, x, norm_g3, mod_arr, mod_specs, cond_row, w_main, w_tail, tm):
    rows = x.shape[0]
    tn = 2 * D_CONV if tm >= 1024 else D_CONV
    sc_spec, sh_spec = mod_specs
    return pl.pallas_call(
        functools.partial(_inproj_kernel, cond_row=cond_row),
        grid=(rows // tm, D_MAIN // tn),
        in_specs=[
            pl.BlockSpec((tm, D_MODEL), lambda i, j: (i, 0)),
            pl.BlockSpec((None, 1, D_MODEL), lambda i, j: (layer, 0, 0)),
            sc_spec,
            sh_spec,
            pl.BlockSpec((None, D_MODEL, tn), lambda i, j: (layer, 0, j)),
            pl.BlockSpec((None, 2 * HEADS, D_MODEL), lambda i, j: (layer, 0, 0)),
        ],
        out_specs=[
            pl.BlockSpec((tm, tn), lambda i, j: (i, j)),
            pl.BlockSpec((tm, LANES), lambda i, j: (i, 0)),
        ],
        out_shape=[
            jax.ShapeDtypeStruct((rows, D_MAIN), F32),
            jax.ShapeDtypeStruct((rows, LANES), F32),
        ],
        scratch_shapes=[pltpu.VMEM((tm, D_MODEL), BF16)],
        compiler_params=pltpu.CompilerParams(
            dimension_semantics=("arbitrary", "arbitrary"), vmem_limit_bytes=VMEM_LIMIT_BIG_TILES),
        name="in_proj",
    )(x, norm_g3, mod_arr, mod_arr, w_main, w_tail)


def _outproj_kernel(ya_ref, yb_ref, x_ref, gate_ref, wa_ref, wb_ref, fg_ref, o_ref, *, final,
                    cond_row):
    acc = (_dot(ya_ref[...].astype(BF16), wa_ref[...])
           + _dot(yb_ref[...].astype(BF16), wb_ref[...]))
    xn = x_ref[...] + _cond_rows(gate_ref, cond_row) * acc
    if final:
        xn = xn * lax.rsqrt(jnp.mean(xn * xn, axis=-1, keepdims=True) + EPS) * fg_ref[...]
    o_ref[...] = xn


def _outproj_call(layer, ya, yb, y_specs, x, mod_arr, gate_spec, cond_row, w_out, final_g, tm):
    rows = x.shape[0]
    half = D_MODEL // 2
    return pl.pallas_call(
        functools.partial(_outproj_kernel, final=(layer == DEPTH - 1), cond_row=cond_row),
        grid=(rows // tm,),
        in_specs=[
            *y_specs,
            pl.BlockSpec((tm, D_MODEL), lambda i: (i, 0)),
            gate_spec,
            pl.BlockSpec((None, half, D_MODEL), lambda i: (layer, 0, 0), pipeline_mode=pl.Buffered(1)),
            pl.BlockSpec((None, half, D_MODEL), lambda i: (layer, 1, 0), pipeline_mode=pl.Buffered(1)),
            pl.BlockSpec((1, D_MODEL), lambda i: (0, 0)),
        ],
        out_specs=pl.BlockSpec((tm, D_MODEL), lambda i: (i, 0)),
        out_shape=jax.ShapeDtypeStruct((rows, D_MODEL), F32),
        compiler_params=pltpu.CompilerParams(
            dimension_semantics=("arbitrary",), vmem_limit_bytes=VMEM_LIMIT_BIG_TILES),
        name="out_proj",
    )(ya, yb, x, mod_arr, w_out, w_out, final_g)


def _gate_terms(tail, alog_row, dtb_row):
    beta = jax.nn.sigmoid(tail)
    g = -jnp.exp(alog_row) * _softplus(tail + dtb_row)
    return beta, g


def _bdot(a, b):
    return lax.dot_general(a, b, (((2,), (1,)), ((0,), (0,))), preferred_element_type=F32)


def _bdot_nt(a, b):
    return lax.dot_general(a, b, (((2,), (2,)), ((0,), (0,))), preferred_element_type=F32)


def _bdot_tn(a, b):
    return lax.dot_general(a, b, (((1,), (1,)), ((0,), (0,))), preferred_element_type=F32)


def _split_bf16(a):
    hi = a.astype(BF16)
    lo = (a - hi.astype(F32)).astype(BF16)
    return hi, lo


def _unit_lower_inverse_levels(lmat, eye):
    heads, c, _ = lmat.shape
    eye_b = jnp.broadcast_to(eye, lmat.shape)
    pass_x = jnp.concatenate([eye_b, jnp.zeros_like(eye_b)], axis=2).astype(BF16)
    zeros_top = jnp.zeros((heads, c, 2 * c), BF16)
    w = jnp.concatenate([eye_b, -lmat], axis=2)
    for _ in range(6):
        w_hi, w_lo = _split_bf16(w)
        lhs = jnp.concatenate([w_hi, w_lo, w_hi], axis=2)
        rhs = jnp.concatenate([pass_x, w_hi, pass_x, w_hi, zeros_top, w_lo], axis=1)
        w = _bdot(lhs, rhs)
        yield w[:, :, :c], w


class _ChunkOperands(NamedTuple):
    k16: object
    kb16: object
    q16: object
    rhs16: object
    qeg16: object
    kt16: object
    dec: object
    sdec: object
    gate: object


def _operand_scratch(ns):
    nh = ns * HEADS
    return _ChunkOperands(
        k16=pltpu.VMEM((nh, CHUNK, HEAD_DIM), BF16),
        kb16=pltpu.VMEM((nh, CHUNK, HEAD_DIM), BF16),
        q16=pltpu.VMEM((nh, CHUNK, HEAD_DIM), BF16),
        rhs16=pltpu.VMEM((nh, CHUNK, 2 * HEAD_DIM), BF16),
        qeg16=pltpu.VMEM((nh, CHUNK, HEAD_DIM), BF16),
        kt16=pltpu.VMEM((nh, CHUNK, HEAD_DIM), BF16),
        dec=pltpu.VMEM((nh, CHUNK, CHUNK), F32),
        sdec=pltpu.VMEM((nh, 1, HEAD_DIM), F32),
        gate=pltpu.VMEM((ns, CHUNK, D_GDN), F32),
    )


def _prep_tasks(z_ref, zt_ref, wa_ref, wq_ref, alog_ref, dtb_ref, ya_ref, ca_ref, cq_ref,
                buf_a, buf_q, out):
    C = CHUNK
    ns = z_ref.shape[0]
    row = lax.broadcasted_iota(jnp.int32, (C, C), 0)
    col = lax.broadcasted_iota(jnp.int32, (C, C), 1)
    causal = row >= col
    tri = causal.astype(F32)
    sub = lax.broadcasted_iota(jnp.int32, (C // 8, 8, HEAD_DIM), 1)
    never = pl.program_id(0) < 0

    def pinned(row_vec, token):
        return row_vec if token is None else jnp.where(never, token, row_vec)

    def shifted(x3, d):
        r = pltpu.roll(x3, d, axis=1)
        return jnp.where(sub < d, r[:-1], r[1:]).reshape(C, x3.shape[-1])

    def haloed(buf, s, sl):
        x = buf[s, :, sl]
        return x, x.reshape((HALO + C) // 8, 8, HEAD_DIM)

    gates = {}

    def gate_task(s):
        def run(token):
            del token
            beta_s, g_s = _gate_terms(zt_ref[s], alog_ref[...], dtb_ref[...])
            gc_s = _dot_hi(tri, g_s)
            gates[s] = (beta_s, gc_s, gc_s.T)
            buf_a[s, HALO:HALO + C, :] = (z_ref[s, :, OFF_CA:OFF_CA + D_CONV]
                                          * z_ref[s, :, OFF_HA:OFF_HA + D_CONV])
            ca_ref[s] = buf_a[s, HALO + C - 2:HALO + C, :]
            buf_q[s, HALO:HALO + C, :] = z_ref[s, :, OFF_QKV:OFF_QKV + 3 * D_GDN]
            cq_ref[s] = buf_q[s, HALO + C - 3:HALO + C, :]
        return run

    def conv_a_task(s, off):
        def run(token):
            sl = slice(off, off + HEAD_DIM)
            xa, xa3 = haloed(buf_a, s, sl)
            conv = (shifted(xa3, 2) * pinned(wa_ref[0:1, sl], token)
                    + shifted(xa3, 1) * pinned(wa_ref[1:2, sl], token)
                    + xa[HALO:] * pinned(wa_ref[2:3, sl], token))
            ya = (z_ref[s, :, OFF_BA + off:OFF_BA + off + HEAD_DIM] * conv
                  * z_ref[s, :, OFF_GA + off:OFF_GA + off + HEAD_DIM])
            ya_ref[s, :, sl] = ya.astype(ya_ref.dtype)
            buf_a[s, 0:HALO, sl] = xa[C:]
        return run

    def conv_qkv(s, off, token):
        sl = slice(off, off + HEAD_DIM)
        x, x3 = haloed(buf_q, s, sl)
        acc = x[HALO:] * pinned(wq_ref[3:4, sl], token)
        for d in range(1, 4):
            acc = acc + shifted(x3, d) * pinned(wq_ref[3 - d:4 - d, sl], token)
        buf_q[s, 0:HALO, sl] = x[C:]
        return _silu_of_half(acc)

    def head_task(s, h):
        def run(token):
            n = s * HEADS + h
            beta_s, gc_s, gc_t = gates[s]
            sl = slice(h * HEAD_DIM, (h + 1) * HEAD_DIM)
            q = conv_qkv(s, h * HEAD_DIM, token)
            k = conv_qkv(s, D_GDN + h * HEAD_DIM, token)
            v = conv_qkv(s, 2 * D_GDN + h * HEAD_DIM, token)
            q = q * (lax.rsqrt(jnp.sum(q * q, axis=-1, keepdims=True) + EPS) * (HEAD_DIM ** -0.5))
            k = k * lax.rsqrt(jnp.sum(k * k, axis=-1, keepdims=True) + EPS)
            beta = jnp.broadcast_to(beta_s[:, h:h + 1], (C, HEAD_DIM))
            gcol = jnp.broadcast_to(gc_s[:, HEADS + h:HEADS + h + 1], (C, HEAD_DIM))
            grow = gc_t[HEADS + h:HEADS + h + 1, :]
            glast = gcol[C - 1:C, :]
            eg = jnp.exp(gcol)
            kb = k * beta
            out.k16[n] = k.astype(BF16)
            out.kb16[n] = kb.astype(BF16)
            out.q16[n] = q.astype(BF16)
            out.rhs16[n] = jnp.concatenate([v * beta, kb * eg], axis=1).astype(BF16)
            out.qeg16[n] = (q * eg).astype(BF16)
            out.kt16[n] = (k * jnp.exp(glast - gcol)).astype(BF16)
            out.dec[n] = jnp.exp(jnp.where(causal, gcol[:, :C] - grow, -jnp.inf))
            out.sdec[n] = jnp.exp(glast)
            out.gate[s, :, sl] = z_ref[s, :, OFF_GB + h * HEAD_DIM:OFF_GB + (h + 1) * HEAD_DIM]
        return run

    tasks = [gate_task(s) for s in range(ns)]
    for s in range(ns):
        for h in range(HEADS):
            tasks.append(head_task(s, h))
            tasks.append(conv_a_task(s, h * HEAD_DIM))
    return tasks


def _scan_stages(ops, on_ref, yb_ref, s_scr):
    C = CHUNK
    ns = yb_ref.shape[0]
    row = lax.broadcasted_iota(jnp.int32, (C, C), 0)
    col = lax.broadcasted_iota(jnp.int32, (C, C), 1)
    strict = (row > col)[None]
    eye = (row == col).astype(F32)[None]

    k16 = ops.k16[...]
    dec_causal = ops.dec[...]
    lmat = _bdot_nt(ops.kb16[...], k16) * jnp.where(strict, dec_causal, 0.0)
    attn = (_bdot_nt(ops.q16[...], k16) * dec_causal).astype(BF16)
    nh = k16.shape[0]
    token_of = lambda a: a[nh - 1, C - 1:C, :HEAD_DIM]
    yield None
    for tmat, w_level in _unit_lower_inverse_levels(lmat, eye):
        yield token_of(w_level)
    uw = _bdot(tmat.astype(BF16), ops.rhs16[...])
    yield token_of(uw)
    s_old = s_scr[...]
    lhs = jnp.concatenate([uw[:, :, HEAD_DIM:].astype(BF16), ops.qeg16[...]], axis=1)
    ws = _bdot(lhs, s_old.astype(BF16))
    yield token_of(ws)
    u = uw[:, :, :HEAD_DIM] - ws[:, :C]
    u16 = u.astype(BF16)
    o = ws[:, C:] + _bdot(attn, u16)
    s_scr[...] = s_old * ops.sdec[...] + _bdot_tn(ops.kt16[...], u16)
    yield token_of(o)
    o = o * lax.rsqrt(jnp.mean(o * o, axis=-1, keepdims=True) + EPS) * on_ref[...]
    for s in range(ns):
        for h in range(HEADS):
            sl = slice(h * HEAD_DIM, (h + 1) * HEAD_DIM)
            yb_ref[s, :, sl] = (o[s * HEADS + h] * ops.gate[s, :, sl]).astype(yb_ref.dtype)


def _interleave(stages, tasks, per_gap):
    todo = list(tasks)
    token = None
    for n, token in zip(list(per_gap) + [0] * 64, stages):
        for task in todo[:n]:
            task(token)
        todo = todo[n:]
    for task in todo:
        task(token)


def _mixer_prompt_kernel(z_ref, zt_ref, wa_ref, wq_ref, alog_ref, dtb_ref, on_ref,
                         ya_ref, yb_ref, ca_ref, cq_ref, ss_ref, buf_a, buf_q, s_scr, *operand_refs,
                         chunks_per_seq):
    t = pl.program_id(0)
    ns = z_ref.shape[0]
    n_fields = len(_ChunkOperands._fields)
    sets = (_ChunkOperands(*operand_refs[:n_fields]), _ChunkOperands(*operand_refs[n_fields:]))
    first_of_group = t % chunks_per_seq == 0

    @pl.when(t == 0)
    def _():
        for ref in sets[1]:
            ref[...] = jnp.zeros(ref.shape, ref.dtype)

    @pl.when(first_of_group)
    def _():
        buf_a[:, 0:HALO, :] = jnp.zeros((ns, HALO, D_CONV), F32)
        buf_q[:, 0:HALO, :] = jnp.zeros((ns, HALO, 3 * D_GDN), F32)

    @pl.when((t <= 1) | (t % chunks_per_seq == 1))
    def _():
        s_scr[...] = jnp.zeros(s_scr.shape, F32)

    for parity in range(2):
        @pl.when(t % 2 == parity)
        def _(parity=parity):
            _interleave(
                _scan_stages(sets[1 - parity], on_ref, yb_ref, s_scr),
                _prep_tasks(z_ref, zt_ref, wa_ref, wq_ref, alog_ref, dtb_ref, ya_ref, ca_ref,
                            cq_ref, buf_a, buf_q, sets[parity]),
                PREP_TASKS_PER_GAP)

    @pl.when(first_of_group & (t > 0))
    def _():
        ss_ref[...] = s_scr[...].reshape(ss_ref.shape)


def _skip_refs(n, kernel, *refs):
    kernel(*refs[n:])


def _mixer_prompt_call(layer, z, zt, conv_a_w, conv_qkv_w, alog_rows, dtb_rows, onorm3, batch, seq,
                       stacks):
    nc = seq // CHUNK
    ns = PROMPT_STREAMS
    n_pairs = (batch // ns) * nc
    lsel = lambda t: (layer, 0, 0)
    prep_pair = lambda t: jnp.minimum(t, n_pairs - 1)
    scan_pair = lambda t: jnp.maximum(t - 1, 0)
    cur = lambda t: (prep_pair(t) // nc, prep_pair(t) % nc, 0)
    prev = lambda t: (scan_pair(t) // nc, scan_pair(t) % nc, 0)
    scratch = _operand_scratch(ns)
    chained = () if stacks is None else tuple(stacks)
    n_chain = len(chained)
    return pl.pallas_call(
        functools.partial(_skip_refs, n_chain,
                          functools.partial(_mixer_prompt_kernel, chunks_per_seq=nc)),
        grid=(n_pairs + 1,),
        input_output_aliases={i: 2 + i for i in range(n_chain)},
        in_specs=[pl.BlockSpec(memory_space=pl.ANY)] * n_chain + [
            pl.BlockSpec((ns, CHUNK, D_MAIN), cur),
            pl.BlockSpec((ns, CHUNK, LANES), cur),
            pl.BlockSpec((None, 3, D_CONV), lsel),
            pl.BlockSpec((None, 4, 3 * D_GDN), lsel),
            pl.BlockSpec((None, 1, LANES), lsel),
            pl.BlockSpec((None, 1, LANES), lsel),
            pl.BlockSpec((None, 1, HEAD_DIM), lsel),
        ],
        out_specs=[
            pl.BlockSpec((ns, CHUNK, D_CONV),
                         lambda t: (prep_pair(t) // nc, t - (prep_pair(t) // nc) * nc, 0)),
            pl.BlockSpec((ns, CHUNK, D_GDN), prev),
            pl.BlockSpec((None, ns, 2, D_CONV), lambda t: (layer, prep_pair(t) // nc, 0, 0)),
            pl.BlockSpec((None, ns, 3, 3 * D_GDN), lambda t: (layer, prep_pair(t) // nc, 0, 0)),
            pl.BlockSpec((None, ns, HEADS, HEAD_DIM, HEAD_DIM),
                         lambda t: (layer, scan_pair(t) // nc, 0, 0, 0)),
        ],
        out_shape=[
            jax.ShapeDtypeStruct((batch, seq + CHUNK, D_CONV), BF16),
            jax.ShapeDtypeStruct((batch, seq, D_GDN), BF16),
            jax.ShapeDtypeStruct((DEPTH, batch, 2, D_CONV), F32),
            jax.ShapeDtypeStruct((DEPTH, batch, 3, 3 * D_GDN), F32),
            jax.ShapeDtypeStruct((DEPTH, batch, HEADS, HEAD_DIM, HEAD_DIM), F32),
        ],
        scratch_shapes=[
            pltpu.VMEM((ns, HALO + CHUNK, D_CONV), F32),
            pltpu.VMEM((ns, HALO + CHUNK, 3 * D_GDN), F32),
            pltpu.VMEM((ns * HEADS, HEAD_DIM, HEAD_DIM), F32),
            *scratch, *scratch,
        ],
        compiler_params=pltpu.CompilerParams(
            dimension_semantics=("arbitrary",), vmem_limit_bytes=VMEM_LIMIT),
        name="mixer_prompt",
    )(*chained, z.reshape(batch, seq, D_MAIN), zt.reshape(batch, seq, LANES), conv_a_w, conv_qkv_w,
      alog_rows, dtb_rows, onorm3)


def _mixer_sample_kernel(z_ref, zt_ref, sa_ref, sq_ref, s_ref, wa_ref, wq_ref, alog_ref, dtb_ref,
                         on_ref, y_ref, na_ref, nq_ref, ns_ref):
    nb = z_ref.shape[0]

    ch = z_ref[:, OFF_CA:OFF_CA + D_CONV] * z_ref[:, OFF_HA:OFF_HA + D_CONV]
    prev0 = sa_ref[:, 0:D_CONV]
    prev1 = sa_ref[:, D_CONV:2 * D_CONV]
    conv = prev0 * wa_ref[0:1, :] + prev1 * wa_ref[1:2, :] + ch * wa_ref[2:3, :]
    y_ref[:, 0:D_CONV] = (z_ref[:, OFF_BA:OFF_BA + D_CONV] * conv
                          * z_ref[:, OFF_GA:OFF_GA + D_CONV])
    na_ref[:, 0:D_CONV] = prev1
    na_ref[:, D_CONV:2 * D_CONV] = ch

    w3 = 3 * D_GDN
    nq_ref[0] = sq_ref[1]
    nq_ref[1] = sq_ref[2]
    nq_ref[2] = z_ref[:, OFF_QKV:OFF_QKV + w3]

    beta_all, g_all = _gate_terms(zt_ref[...], alog_ref[...], dtb_ref[...])
    eg_all = jnp.exp(g_all)
    rowid = lax.broadcasted_iota(jnp.int32, (nb, HEAD_DIM), 0)

    def conv_qkv(off):
        sl = slice(off, off + HEAD_DIM)
        acc = z_ref[:, OFF_QKV + off:OFF_QKV + off + HEAD_DIM] * wq_ref[3:4, sl]
        for j in range(3):
            acc = acc + sq_ref[j, :, sl] * wq_ref[j:j + 1, sl]
        return _silu_of_half(acc)

    qn, kn, vv, betas, egs = [], [], [], [], []
    for h in range(HEADS):
        q = conv_qkv(h * HEAD_DIM)
        k = conv_qkv(D_GDN + h * HEAD_DIM)
        v = conv_qkv(2 * D_GDN + h * HEAD_DIM)
        qn.append(q * (lax.rsqrt(jnp.sum(q * q, axis=-1, keepdims=True) + EPS) * (HEAD_DIM ** -0.5)))
        kn.append(k * lax.rsqrt(jnp.sum(k * k, axis=-1, keepdims=True) + EPS))
        vv.append(v)
        betas.append(beta_all[:, h:h + 1])
        egs.append(eg_all[:, HEADS + h:HEADS + h + 1])

    pairs = [(j, h) for j in range(nb) for h in range(HEADS)]
    s_old = s_ref[...].reshape(nb * HEADS, HEAD_DIM, HEAD_DIM)
    kq = jnp.stack([jnp.concatenate([kn[h], qn[h]], axis=0) for _, h in pairs], axis=0)
    r = _bdot(kq, s_old)
    us = []
    for h in range(HEADS):
        ks = jnp.zeros((nb, HEAD_DIM), F32)
        qs = jnp.zeros((nb, HEAD_DIM), F32)
        for j in range(nb):
            rp = r[j * HEADS + h]
            ks = jnp.where(rowid == j, rp[:nb], ks)
            qs = jnp.where(rowid == j, rp[nb:], qs)
        u = betas[h] * (vv[h] - egs[h] * ks)
        o = egs[h] * qs + jnp.sum(qn[h] * kn[h], axis=-1, keepdims=True) * u
        o = o * lax.rsqrt(jnp.mean(o * o, axis=-1, keepdims=True) + EPS) * on_ref[...]
        gb = z_ref[:, OFF_GB + h * HEAD_DIM:OFF_GB + (h + 1) * HEAD_DIM]
        y_ref[:, D_CONV + h * HEAD_DIM:D_CONV + (h + 1) * HEAD_DIM] = o * gb
        us.append(u)
    k_own = jnp.stack([jnp.where(rowid == j, kn[h], 0.0) for j, h in pairs], axis=0)
    u_all = jnp.stack([us[h] for _, h in pairs], axis=0)
    decay = jnp.stack([jnp.broadcast_to(egs[h][j:j + 1, :], (1, HEAD_DIM))
                       for j, h in pairs], axis=0)
    s_new = s_old * decay + _bdot_tn(k_own, u_all)
    ns_ref[...] = s_new.reshape(ns_ref.shape)


def _mixer_sample_call(layer, z, zt, sa_all, sq_all, state_ssm, conv_a_w, conv_qkv_w, alog_rows,
                       dtb_rows, onorm3, stacks):
    nseq = z.shape[0]
    nb = SAMPLE_SEQS_PER_STEP
    lsel = lambda i: (layer, 0, 0)
    chained = () if stacks is None else tuple(stacks)
    n_chain = len(chained)
    return pl.pallas_call(
        functools.partial(_skip_refs, n_chain, _mixer_sample_kernel),
        grid=(nseq // nb,),
        input_output_aliases={i: 1 + i for i in range(n_chain)},
        in_specs=[pl.BlockSpec(memory_space=pl.ANY)] * n_chain + [
            pl.BlockSpec((nb, D_MAIN), lambda i: (i, 0)),
            pl.BlockSpec((nb, LANES), lambda i: (i, 0)),
            pl.BlockSpec((None, nb, 2 * D_CONV), lambda i: (layer, i, 0)),
            pl.BlockSpec((None, 3, nb, 3 * D_GDN), lambda i: (layer, 0, i, 0)),
            pl.BlockSpec((None, nb, HEADS, HEAD_DIM, HEAD_DIM), lambda i: (layer, i, 0, 0, 0)),
            pl.BlockSpec((None, 3, D_CONV), lsel),
            pl.BlockSpec((None, 4, 3 * D_GDN), lsel),
            pl.BlockSpec((None, 1, LANES), lsel),
            pl.BlockSpec((None, 1, LANES), lsel),
            pl.BlockSpec((None, 1, HEAD_DIM), lsel),
        ],
        out_specs=[
            pl.BlockSpec((nb, D_MODEL), lambda i: (i, 0)),
            pl.BlockSpec((None, nb, 2 * D_CONV), lambda i: (layer, i, 0)),
            pl.BlockSpec((None, 3, nb, 3 * D_GDN), lambda i: (layer, 0, i, 0)),
            pl.BlockSpec((None, nb, HEADS, HEAD_DIM, HEAD_DIM), lambda i: (layer, i, 0, 0, 0)),
        ],
        out_shape=[
            jax.ShapeDtypeStruct((nseq, D_MODEL), F32),
            jax.ShapeDtypeStruct((DEPTH, nseq, 2 * D_CONV), F32),
            jax.ShapeDtypeStruct((DEPTH, 3, nseq, 3 * D_GDN), F32),
            jax.ShapeDtypeStruct((DEPTH, nseq, HEADS, HEAD_DIM, HEAD_DIM), F32),
        ],
        compiler_params=pltpu.CompilerParams(
            dimension_semantics=("arbitrary",), vmem_limit_bytes=VMEM_LIMIT),
        name="mixer_sample",
    )(*chained, z, zt, sa_all, sq_all, state_ssm, conv_a_w, conv_qkv_w, alog_rows, dtb_rows, onorm3)


def kernel(x_prompt, x_sample, state_conv_a, state_conv_qkv, state_ssm, c_prompt, c_sample, norm_g, w_ada, b_ada, w_in, conv_a_w, conv_qkv_w, a_log, dt_bias, o_norm_g, w_out, final_norm_g):
    batch, seq, _ = x_prompt.shape
    nseq = x_sample.shape[0]
    assert x_sample.shape[1] == 1 and seq % CHUNK == 0

    w_in_t = jnp.swapaxes(w_in, 1, 2)
    w_main = _cast_transpose_call(w_in_t, D_MAIN, 2048)
    w_tail = w_in_t[:, D_MAIN:, :].astype(BF16)
    w_out16 = w_out.astype(BF16)
    norm_g3 = norm_g.reshape(DEPTH, 1, D_MODEL)
    onorm3 = o_norm_g.reshape(DEPTH, 1, HEAD_DIM)
    final_g = final_norm_g.reshape(1, D_MODEL)
    conv_qkv_half = 0.5 * conv_qkv_w
    pad_heads = ((0, 0), (HEADS, LANES - 2 * HEADS))
    alog_rows = jnp.pad(a_log, pad_heads).reshape(DEPTH, 1, LANES)
    dtb_rows = jnp.pad(dt_bias, pad_heads).reshape(DEPTH, 1, LANES)

    n_cond = nseq + batch
    n_cond_pad = -(-n_cond // 8) * 8
    c_all = jnp.concatenate(
        [c_sample, c_prompt, jnp.zeros((n_cond_pad - n_cond, D_MODEL), F32)], axis=0)
    mod = _mod_call(c_all, w_ada, b_ada)

    xp = x_prompt.reshape(batch * seq, D_MODEL)
    xs = x_sample.reshape(nseq, D_MODEL)
    tm_in = 1024
    tm_out = 1024
    sa_all = state_conv_a.reshape(DEPTH, nseq, 2 * D_CONV)
    sq_all = jnp.swapaxes(state_conv_qkv, 1, 2)
    prompt_states = sample_states = None

    for layer in range(DEPTH):
        def pmod(sec, per_batch, layer=layer):
            return pl.BlockSpec((None, 8, D_MODEL),
                                lambda i, *_: (layer, (nseq + i // per_batch) // 8, sec))

        def prow(per_batch):
            return lambda i: (nseq + i // per_batch) % 8
        z, zt = _inproj_call(layer, xp, norm_g3, mod, (pmod(1, seq // tm_in), pmod(0, seq // tm_in)),
                             prow(seq // tm_in), w_main, w_tail, tm_in)
        ya, yb, *prompt_states = _mixer_prompt_call(layer, z, zt, conv_a_w, conv_qkv_half, alog_rows,
                                                    dtb_rows, onorm3, batch, seq, prompt_states)
        per_seq = seq // tm_out
        y_specs = [pl.BlockSpec((None, tm_out, D_CONV), lambda i: (i // per_seq, i % per_seq, 0)),
                   pl.BlockSpec((None, tm_out, D_GDN), lambda i: (i // per_seq, i % per_seq, 0))]
        xp = _outproj_call(layer, ya, yb, y_specs, xp, mod, pmod(2, per_seq), prow(per_seq), w_out16,
                           final_g, tm_out)

        def smod(sec, layer=layer):
            return pl.BlockSpec((None, nseq, D_MODEL), lambda i, *_: (layer, 0, sec))
        z, zt = _inproj_call(layer, xs, norm_g3, mod, (smod(1), smod(0)), None, w_main, w_tail, nseq)
        y, *sample_states = _mixer_sample_call(layer, z, zt, sa_all, sq_all, state_ssm, conv_a_w,
                                               conv_qkv_half, alog_rows, dtb_rows, onorm3,
                                               sample_states)
        y_specs = [pl.BlockSpec((nseq, D_CONV), lambda i: (0, 0)),
                   pl.BlockSpec((nseq, D_GDN), lambda i: (0, 1))]
        xs = _outproj_call(layer, y, y, y_specs, xs, mod, smod(2), None, w_out16, final_g, nseq)

    ca_p, cq_p, ss_p = prompt_states
    na, nq, ss_s = sample_states
    return (xp.reshape(batch, seq, D_MODEL), xs.reshape(nseq, 1, D_MODEL), ca_p, cq_p, ss_p,
            na.reshape(DEPTH, nseq, 2, D_CONV), jnp.swapaxes(nq, 1, 2), ss_s)
```
